```python
import jax, jax.numpy as jnp
from jax import lax
import numpy as np

D_MODEL = 1024
BATCH = 8
SEQ = 4096
DEPTH = 1

D_MIX = D_MODEL
D_POOL = D_MIX // 2
D_DN = D_MIX - D_POOL
POOL_WINDOWS = (2, 4, 8, 16)
N_POOL_GROUPS = len(POOL_WINDOWS)
POOL_GROUP = D_POOL // N_POOL_GROUPS
DN_HEAD_DIM = 128
DN_HEADS = D_DN // DN_HEAD_DIM
CONV_WIDTH = 4
CHUNK = 64
NORM_EPS = 1e-6
SPLIT_SIZES = (D_POOL, D_POOL, D_DN, D_DN, D_DN, D_DN, DN_HEADS, DN_HEADS)
D_IN = sum(SPLIT_SIZES)

kernel_name = "hymba_pool_gated_deltanet_block"


def rms_norm(x, w):
    xf = x.astype(jnp.float32)
    y = xf * lax.rsqrt(jnp.mean(xf * xf, axis=-1, keepdims=True) + NORM_EPS)
    return (y * w.astype(jnp.float32)).astype(x.dtype)


def l2_normalize(t):
    return t * lax.rsqrt(jnp.sum(t * t, axis=-1, keepdims=True) + NORM_EPS)


def pool_mixer(u, z, pool_w, pool_scale):
    B, S, _ = u.shape
    uf = u.astype(jnp.float32).reshape(B, S, N_POOL_GROUPS, POOL_GROUP)
    csum = jnp.cumsum(uf, axis=1)
    counts = jnp.arange(1, S + 1, dtype=jnp.float32)
    outs = []
    for gi, w in enumerate(POOL_WINDOWS):
        c = csum[:, :, gi]
        prev = jnp.pad(c, ((0, 0), (w, 0), (0, 0)))[:, :S]
        cnt = jnp.minimum(counts, float(w))[None, :, None]
        outs.append((c - prev) / cnt)
    mix = jnp.stack(outs, axis=2) - uf
    mix = jnp.einsum('bsgc,gcd->bsgd', mix, pool_w.astype(jnp.float32)).reshape(B, S, D_POOL)
    out = mix * pool_scale.astype(jnp.float32) * jax.nn.silu(z.astype(jnp.float32))
    return out.astype(u.dtype)


def causal_depthwise_conv(u, w):
    K, C = w.shape
    return lax.conv_general_dilated(
        u, w[:, None, :], window_strides=(1,), padding=[(K - 1, 0)],
        dimension_numbers=('NWC', 'WIO', 'NWC'), feature_group_count=C)


def gated_delta_rule(q, k, v, g, beta):
    B, H, S, dk = q.shape
    dv = v.shape[-1]
    n = S // CHUNK
    q = q * (dk ** -0.5)
    k_beta = k * beta[..., None]
    v_beta = v * beta[..., None]
    chunked = lambda t: t.reshape(B, H, n, CHUNK, t.shape[-1])
    q, k, k_beta, v_beta = chunked(q), chunked(k), chunked(k_beta), chunked(v_beta)
    gc = jnp.cumsum(g.reshape(B, H, n, CHUNK), axis=-1)
    causal = jnp.tril(jnp.ones((CHUNK, CHUNK), dtype=bool))
    strict = jnp.tril(jnp.ones((CHUNK, CHUNK), dtype=bool), k=-1)
    diff = gc[..., :, None] - gc[..., None, :]
    decay = jnp.exp(jnp.where(causal, diff, -jnp.inf))
    A = jnp.where(strict, jnp.einsum('bhncd,bhnmd->bhncm', k_beta, k) * decay, 0.0)
    eye = jnp.eye(CHUNK, dtype=jnp.float32)
    T = lax.linalg.triangular_solve(A + eye, jnp.broadcast_to(eye, A.shape),
                                    left_side=True, lower=True, unit_diagonal=True)
    u = jnp.einsum('bhncm,bhnmd->bhncd', T, v_beta)
    w = jnp.einsum('bhncm,bhnmd->bhncd', T, k_beta * jnp.exp(gc)[..., None])
    qk = jnp.einsum('bhncd,bhnmd->bhncm', q, k) * decay
    q_dec = q * jnp.exp(gc)[..., None]
    k_dec = k * jnp.exp(gc[..., -1:] - gc)[..., None]
    chunk_decay = jnp.exp(gc[..., -1])

    def step(state, xs):
        qk_i, q_dec_i, k_dec_i, u_i, w_i, dec_i = xs
        v_new = u_i - jnp.einsum('bhcd,bhde->bhce', w_i, state)
        o = jnp.einsum('bhcd,bhde->bhce', q_dec_i, state) + jnp.einsum('bhcm,bhme->bhce', qk_i, v_new)
        state = state * dec_i[..., None, None] + jnp.einsum('bhcd,bhce->bhde', k_dec_i, v_new)
        return state, o

    to_scan = lambda t: jnp.moveaxis(t, 2, 0)
    xs = (to_scan(qk), to_scan(q_dec), to_scan(k_dec), to_scan(u), to_scan(w), to_scan(chunk_decay))
    state0 = jnp.zeros((B, H, dk, dv), dtype=jnp.float32)
    _, o = lax.scan(step, state0, xs)
    return jnp.moveaxis(o, 0, 2).reshape(B, H, S, dv)


def deltanet_mixer(q, k, v, z, b, a, conv_w, a_log, dt_bias, norm_w):
    B, S, _ = q.shape
    out_dtype = q.dtype
    qkv = jnp.concatenate([q, k, v], axis=-1).astype(jnp.float32)
    qkv = jax.nn.silu(causal_depthwise_conv(qkv, conv_w.astype(jnp.float32)))
    q, k, v = jnp.split(qkv, 3, axis=-1)
    heads = lambda t: t.reshape(B, S, DN_HEADS, DN_HEAD_DIM).transpose(0, 2, 1, 3)
    q, k, v = l2_normalize(heads(q)), l2_normalize(heads(k)), heads(v)
    beta = jax.nn.sigmoid(b.astype(jnp.float32)).transpose(0, 2, 1)
    g = (-jnp.exp(a_log.astype(jnp.float32))
         * jax.nn.softplus(a.astype(jnp.float32) + dt_bias.astype(jnp.float32))).transpose(0, 2, 1)
    o = gated_delta_rule(q, k, v, g, beta).transpose(0, 2, 1, 3)
    o = o * lax.rsqrt(jnp.mean(o * o, axis=-1, keepdims=True) + NORM_EPS) * norm_w.astype(jnp.float32)
    o = o * jax.nn.silu(z.astype(jnp.float32).reshape(B, S, DN_HEADS, DN_HEAD_DIM))
    return o.reshape(B, S, D_DN).astype(out_dtype)


def setup_inputs(seed: int = 0) -> dict:
    key = jax.random.key(seed)
    ks = jax.random.split(key, 12)
    f32 = jnp.float32
    x = jax.random.normal(ks[0], (BATCH, SEQ, D_MODEL), f32)
    norm_w = 1.0 + 0.02 * jax.random.normal(ks[1], (DEPTH, D_MODEL), f32)
    w_in = jax.random.normal(ks[2], (DEPTH, D_MODEL, D_IN), f32) * D_MODEL ** -0.5
    pool_w = jax.random.normal(ks[3], (DEPTH, N_POOL_GROUPS, POOL_GROUP, POOL_GROUP), f32) * POOL_GROUP ** -0.5
    pool_scale = 1.0 + 0.1 * jax.random.normal(ks[4], (DEPTH, D_POOL), f32)
    conv_w = jax.random.normal(ks[5], (DEPTH, CONV_WIDTH, 3 * D_DN), f32) * CONV_WIDTH ** -0.5
    a_log = jnp.log(jax.random.uniform(ks[6], (DEPTH, DN_HEADS), f32, 1.0, 16.0))
    dt = jnp.exp(jax.random.uniform(ks[7], (DEPTH, DN_HEADS), f32, np.log(1e-3), np.log(1e-1)))
    dt_bias = dt + jnp.log(-jnp.expm1(-dt))
    dn_norm_w = 1.0 + 0.02 * jax.random.normal(ks[8], (DEPTH, DN_HEAD_DIM), f32)
    w_out = jax.random.normal(ks[9], (DEPTH, D_MIX, D_MODEL), f32) * D_MIX ** -0.5
    final_norm_w = 1.0 + 0.02 * jax.random.normal(ks[10], (D_MODEL,), f32)
    return {"x": x, "norm_w": norm_w, "w_in": w_in, "pool_w": pool_w, "pool_scale": pool_scale,
            "conv_w": conv_w, "a_log": a_log, "dt_bias": dt_bias, "dn_norm_w": dn_norm_w,
            "w_out": w_out, "final_norm_w": final_norm_w}


def reference(x, norm_w, w_in, pool_w, pool_scale, conv_w, a_log, dt_bias, dn_norm_w, w_out, final_norm_w):
    offsets = np.cumsum((0,) + SPLIT_SIZES)
    h = x
    for layer in range(DEPTH):
        n = rms_norm(h, norm_w[layer])
        proj = jnp.einsum('bsd,de->bse', n, w_in[layer])
        pu, pz, q, k, v, dz, b, a = [proj[..., int(offsets[i]):int(offsets[i + 1])]
                                     for i in range(len(SPLIT_SIZES))]
        y_pool = pool_mixer(pu, pz, pool_w[layer], pool_scale[layer])
        y_dn = deltanet_mixer(q, k, v, dz, b, a, conv_w[layer], a_log[layer],
                              dt_bias[layer], dn_norm_w[layer])
        y = jnp.concatenate([y_pool, y_dn], axis=-1)
        h = h + jnp.einsum('bse,ed->bsd', y, w_out[layer])
    return rms_norm(h, final_norm_w)
```

```python
import jax
import jax.numpy as jnp
from jax import lax
from jax.experimental import pallas as pl
from jax.experimental.pallas import tpu as pltpu

D_MODEL = 1024
D_POOL = 512
D_DN = 512
POOL_WINDOWS = (2, 4, 8, 16)
POOL_GROUP = 128
HEAD_DIM = 128
N_HEADS = 4
CONV_WIDTH = 4
NORM_EPS = 1e-6

D_MAIN = 2 * D_POOL + 4 * D_DN
TAIL = 128
OFF_PU, OFF_PZ, OFF_Q, OFF_K, OFF_V, OFF_DZ = 0, 512, 1024, 1536, 2048, 2560
LANE_BETA, LANE_DECAY = 0, N_HEADS

HALO = 16
CHUNK = 64
TS = 256
PROJ_COLS = 512

VMEM_LIMIT_BYTES = 48 * 1024 * 1024

_HI = lax.Precision.HIGHEST


def _bdot(a, b):
    return jnp.dot(a.astype(jnp.bfloat16), b.astype(jnp.bfloat16),
                   preferred_element_type=jnp.float32)


def _bdot_nt(a, b):
    return lax.dot_general(a.astype(jnp.bfloat16), b.astype(jnp.bfloat16),
                           (((1,), (1,)), ((), ())), preferred_element_type=jnp.float32)


def _bdot_tn(a, b):
    return lax.dot_general(a.astype(jnp.bfloat16), b.astype(jnp.bfloat16),
                           (((0,), (0,)), ((), ())), preferred_element_type=jnp.float32)


def _fdot(a, b):
    return jnp.dot(a, b, preferred_element_type=jnp.float32, precision=_HI)


def _silu(x):
    return x * jax.nn.sigmoid(x)


def _unit_lower_inverse(a, row, col):
    n = a.shape[0]
    eye = (row == col).astype(jnp.float32)
    t = eye
    r = 1
    while r < n:
        m = ((row // (2 * r)) == (col // (2 * r))) & ((row % (2 * r)) >= r) & ((col % (2 * r)) < r)
        am = jnp.where(m, a, 0.0)
        if r == 1:
            t = eye - am
        else:
            t = t - _fdot(_fdot(t, am), t)
        r *= 2
    return t


def _block_kernel(x_ref, norm_w_ref, w_main_ref, w_tail_ref, pool_w_ref, pool_scale_ref,
                  conv_w_ref, a_log_ref, dt_bias_ref, dn_norm_w_ref, w_out_ref, fnorm_w_ref,
                  out_ref,
                  p_ref, qkv_ref, gc_ref, gct_ref, beta_ref, y_ref, s_ref):
    sblk = pl.program_id(1)

    @pl.when(sblk == 0)
    def _():
        p_ref[0:HALO, :] = jnp.zeros((HALO, D_MAIN), jnp.float32)
        s_ref[...] = jnp.zeros_like(s_ref)

    @pl.when(sblk != 0)
    def _():
        p_ref[0:HALO, :] = p_ref[TS:TS + HALO, :]

    x = x_ref[0]
    n = x * lax.rsqrt(jnp.mean(x * x, axis=-1, keepdims=True) + NORM_EPS) * norm_w_ref[...]
    nb = n.astype(jnp.bfloat16)
    for c0 in range(0, D_MAIN, PROJ_COLS):
        p_ref[HALO:HALO + TS, c0:c0 + PROJ_COLS] = jnp.dot(
            nb, w_main_ref[:, c0:c0 + PROJ_COLS], preferred_element_type=jnp.float32)
    tail = jnp.dot(nb, w_tail_ref[...], preferred_element_type=jnp.float32)

    t_pos = sblk * TS + lax.broadcasted_iota(jnp.int32, (TS, 1), 0)
    for gi, w in enumerate(POOL_WINDOWS):
        c0 = OFF_PU + gi * POOL_GROUP
        e = p_ref[0:HALO + TS, c0:c0 + POOL_GROUP]
        s = e
        k = 1
        while k < w:
            s = s + pltpu.roll(s, k, axis=0)
            k *= 2
        u = e[HALO:]
        inv_cnt = 1.0 / jnp.minimum(t_pos + 1, w).astype(jnp.float32)
        mix = s[HALO:] * inv_cnt - u
        mixed = _bdot(mix, pool_w_ref[gi])
        z = p_ref[HALO:HALO + TS, OFF_PZ + gi * POOL_GROUP:OFF_PZ + (gi + 1) * POOL_GROUP]
        y_ref[:, c0:c0 + POOL_GROUP] = mixed * pool_scale_ref[:, c0:c0 + POOL_GROUP] * _silu(z)

    for ci in range(3 * D_DN // HEAD_DIM):
        c0 = OFF_Q + ci * HEAD_DIM
        e = p_ref[HALO - 8:HALO + TS, c0:c0 + HEAD_DIM]
        cw = conv_w_ref[:, ci * HEAD_DIM:(ci + 1) * HEAD_DIM]
        acc = e * cw[CONV_WIDTH - 1:CONV_WIDTH, :]
        for j in range(1, CONV_WIDTH):
            acc = acc + pltpu.roll(e, j, axis=0) * cw[CONV_WIDTH - 1 - j:CONV_WIDTH - j, :]
        a = _silu(acc[8:])
        if ci < 2 * N_HEADS:
            a = a * lax.rsqrt(jnp.sum(a * a, axis=-1, keepdims=True) + NORM_EPS)
        if ci < N_HEADS:
            a = a * (HEAD_DIM ** -0.5)
        qkv_ref[:, ci * HEAD_DIM:(ci + 1) * HEAD_DIM] = a

    beta_ref[...] = jax.nn.sigmoid(tail)
    g = -jnp.exp(a_log_ref[...]) * jax.nn.softplus(tail + dt_bias_ref[...])
    r_in_chunk = lax.broadcasted_iota(jnp.int32, (TS, TAIL), 0) % CHUNK
    gc = g
    k = 1
    while k < CHUNK:
        gc = gc + jnp.where(r_in_chunk >= k, pltpu.roll(gc, k, axis=0), 0.0)
        k *= 2
    gc_ref[...] = gc
    gct_ref[...] = gc.T

    row = lax.broadcasted_iota(jnp.int32, (CHUNK, CHUNK), 0)
    col = lax.broadcasted_iota(jnp.int32, (CHUNK, CHUNK), 1)
    for h in range(N_HEADS):
        state = s_ref[h]
        for c in range(TS // CHUNK):
            r0 = c * CHUNK
            lane = LANE_DECAY + h
            q = qkv_ref[r0:r0 + CHUNK, h * HEAD_DIM:(h + 1) * HEAD_DIM]
            kk = qkv_ref[r0:r0 + CHUNK, D_DN + h * HEAD_DIM:D_DN + (h + 1) * HEAD_DIM]
            v = qkv_ref[r0:r0 + CHUNK, 2 * D_DN + h * HEAD_DIM:2 * D_DN + (h + 1) * HEAD_DIM]
            bcol = beta_ref[r0:r0 + CHUNK, LANE_BETA + h:LANE_BETA + h + 1]
            gcol = gc_ref[r0:r0 + CHUNK, lane:lane + 1]
            grow = gct_ref[lane:lane + 1, r0:r0 + CHUNK]
            glast = gc_ref[r0 + CHUNK - 1:r0 + CHUNK, lane:lane + 1]

            k_beta = kk * bcol
            v_beta = v * bcol
            decay = jnp.exp(jnp.where(row >= col, gcol - grow, -jnp.inf))
            a_mat = jnp.where(row > col, _bdot_nt(k_beta, kk) * decay, 0.0)
            t_mat = _unit_lower_inverse(a_mat, row, col)
            egc = jnp.exp(gcol)
            u = _bdot(t_mat, v_beta)
            w_ = _bdot(t_mat, k_beta * egc)
            qk = _bdot_nt(q, kk) * decay
            v_new = u - _bdot(w_, state)
            o = _bdot(q * egc, state) + _bdot(qk, v_new)
            k_dec = kk * jnp.exp(glast - gcol)
            state = state * jnp.exp(glast) + _bdot_tn(k_dec, v_new)

            o = o * lax.rsqrt(jnp.mean(o * o, axis=-1, keepdims=True) + NORM_EPS) * dn_norm_w_ref[...]
            dz = p_ref[HALO + r0:HALO + r0 + CHUNK, OFF_DZ + h * HEAD_DIM:OFF_DZ + (h + 1) * HEAD_DIM]
            y_ref[r0:r0 + CHUNK, D_POOL + h * HEAD_DIM:D_POOL + (h + 1) * HEAD_DIM] = o * _silu(dz)
        s_ref[h] = state

    hres = x + jnp.dot(y_ref[...].astype(jnp.bfloat16), w_out_ref[...],
                       preferred_element_type=jnp.float32)
    out_ref[0] = (hres * lax.rsqrt(jnp.mean(hres * hres, axis=-1, keepdims=True) + NORM_EPS)
                  * fnorm_w_ref[...])


def _lane_row(vals, lane0):
    return jnp.zeros((1, TAIL), jnp.float32).at[0, lane0:lane0 + vals.shape[0]].set(vals)


def kernel(x, norm_w, w_in, pool_w, pool_scale, conv_w, a_log, dt_bias, dn_norm_w, w_out, final_norm_w):
    B, S, D = x.shape
    assert D == D_MODEL and S % TS == 0 and norm_w.shape[0] == 1
    w_in0 = w_in[0]
    w_main = w_in0[:, :D_MAIN].astype(jnp.bfloat16)
    w_tail = jnp.pad(w_in0[:, D_MAIN:], ((0, 0), (0, TAIL - 2 * N_HEADS))).astype(jnp.bfloat16)
    full = lambda shape: pl.BlockSpec(shape, lambda b, s: (0,) * len(shape))
    blk = pl.BlockSpec((1, TS, D_MODEL), lambda b, s: (b, s, 0))
    return pl.pallas_call(
        _block_kernel,
        grid=(B, S // TS),
        in_specs=[
            blk,
            full((1, D_MODEL)),
            full((D_MODEL, D_MAIN)),
            full((D_MODEL, TAIL)),
            full((len(POOL_WINDOWS), POOL_GROUP, POOL_GROUP)),
            full((1, D_POOL)),
            full((CONV_WIDTH, 3 * D_DN)),
            full((1, TAIL)),
            full((1, TAIL)),
            full((1, HEAD_DIM)),
            full((D_MODEL, D_MODEL)),
            full((1, D_MODEL)),
        ],
        out_specs=blk,
        out_shape=jax.ShapeDtypeStruct(x.shape, x.dtype),
        scratch_shapes=[
            pltpu.VMEM((HALO + TS, D_MAIN), jnp.float32),
            pltpu.VMEM((TS, 3 * D_DN), jnp.float32),
            pltpu.VMEM((TS, TAIL), jnp.float32),
            pltpu.VMEM((TAIL, TS), jnp.float32),
            pltpu.VMEM((TS, TAIL), jnp.float32),
            pltpu.VMEM((TS, D_MODEL), jnp.float32),
            pltpu.VMEM((N_HEADS, HEAD_DIM, HEAD_DIM), jnp.float32),
        ],
        compiler_params=pltpu.CompilerParams(
            dimension_semantics=("arbitrary", "arbitrary"),
            vmem_limit_bytes=VMEM_LIMIT_BYTES),
        name="hymba_block",
    )(x, norm_w, w_main, w_tail, pool_w[0].astype(jnp.bfloat16), pool_scale,
      conv_w[0], _lane_row(a_log[0], LANE_DECAY), _lane_row(dt_bias[0], LANE_DECAY),
      dn_norm_w, w_out[0].astype(jnp.bfloat16), final_norm_w[None, :])
```

```python
import jax
import jax.numpy as jnp
from jax import lax
from jax.experimental import pallas as pl
from jax.experimental.pallas import tpu as pltpu

D_MODEL = 1024
D_POOL = 512
D_DN = 512
POOL_WINDOWS = (2, 4, 8, 16)
POOL_GROUP = 128
HEAD_DIM = 128
N_HEADS = 4
CONV_WIDTH = 4
NORM_EPS = 1e-6

D_MAIN = 2 * D_POOL + 4 * D_DN
TAIL = 128
OFF_PU, OFF_PZ, OFF_Q, OFF_K, OFF_V, OFF_DZ = 0, 512, 1024, 1536, 2048, 2560
LANE_BETA, LANE_DECAY = 0, N_HEADS

HALO = 16
CHUNK = 64
TS = 256
N_CHUNKS = TS // CHUNK
PROJ_COLS = 512

VMEM_LIMIT_BYTES = 48 * 1024 * 1024


def _bdot(a, b):
    return jnp.dot(a.astype(jnp.bfloat16), b.astype(jnp.bfloat16),
                   preferred_element_type=jnp.float32)


def _bdot_nt(a, b):
    return lax.dot_general(a.astype(jnp.bfloat16), b.astype(jnp.bfloat16),
                           (((1,), (1,)), ((), ())), preferred_element_type=jnp.float32)


def _bdot_tn(a, b):
    return lax.dot_general(a.astype(jnp.bfloat16), b.astype(jnp.bfloat16),
                           (((0,), (0,)), ((), ())), preferred_element_type=jnp.float32)


def _silu(x):
    return x * jax.nn.sigmoid(x)


def _level_masks(row, col, n):
    masks = []
    r = 1
    while r < n:
        masks.append(((row // (2 * r)) == (col // (2 * r)))
                     & ((row % (2 * r)) >= r) & ((col % (2 * r)) < r))
        r *= 2
    return masks


def _block_kernel(x_ref, norm_w_ref, w_main_ref, w_tail_ref, pool_w_ref, pool_scale_ref,
                  conv_w_ref, a_log_ref, dt_bias_ref, dn_norm_w_ref, w_out_ref, fnorm_w_ref,
                  out_ref,
                  p_ref, qkv_ref, gc_ref, gct_ref, beta_ref, y_ref, s_ref,
                  mq_ref, n_ref, oi_ref, dec_ref):
    sblk = pl.program_id(1)

    @pl.when(sblk == 0)
    def _():
        p_ref[0:HALO, :] = jnp.zeros((HALO, D_MAIN), jnp.float32)
        s_ref[...] = jnp.zeros_like(s_ref)

    @pl.when(sblk != 0)
    def _():
        p_ref[0:HALO, :] = p_ref[TS:TS + HALO, :]

    x = x_ref[0]
    n = x * lax.rsqrt(jnp.mean(x * x, axis=-1, keepdims=True) + NORM_EPS) * norm_w_ref[...]
    nb = n.astype(jnp.bfloat16)
    for c0 in range(0, D_MAIN, PROJ_COLS):
        p_ref[HALO:HALO + TS, c0:c0 + PROJ_COLS] = jnp.dot(
            nb, w_main_ref[:, c0:c0 + PROJ_COLS], preferred_element_type=jnp.float32)
    tail = jnp.dot(nb, w_tail_ref[...], preferred_element_type=jnp.float32)

    t_pos = sblk * TS + lax.broadcasted_iota(jnp.int32, (TS, 1), 0)
    for gi, w in enumerate(POOL_WINDOWS):
        c0 = OFF_PU + gi * POOL_GROUP
        e = p_ref[0:HALO + TS, c0:c0 + POOL_GROUP]
        s = e
        k = 1
        while k < w:
            s = s + pltpu.roll(s, k, axis=0)
            k *= 2
        u = e[HALO:]
        inv_cnt = 1.0 / jnp.minimum(t_pos + 1, w).astype(jnp.float32)
        mix = s[HALO:] * inv_cnt - u
        mixed = _bdot(mix, pool_w_ref[gi])
        z = p_ref[HALO:HALO + TS, OFF_PZ + gi * POOL_GROUP:OFF_PZ + (gi + 1) * POOL_GROUP]
        y_ref[:, c0:c0 + POOL_GROUP] = mixed * pool_scale_ref[:, c0:c0 + POOL_GROUP] * _silu(z)

    for ci in range(3 * D_DN // HEAD_DIM):
        c0 = OFF_Q + ci * HEAD_DIM
        e = p_ref[HALO - 8:HALO + TS, c0:c0 + HEAD_DIM]
        cw = conv_w_ref[:, ci * HEAD_DIM:(ci + 1) * HEAD_DIM]
        acc = e * cw[CONV_WIDTH - 1:CONV_WIDTH, :]
        for j in range(1, CONV_WIDTH):
            acc = acc + pltpu.roll(e, j, axis=0) * cw[CONV_WIDTH - 1 - j:CONV_WIDTH - j, :]
        a = _silu(acc[8:])
        if ci < 2 * N_HEADS:
            a = a * lax.rsqrt(jnp.sum(a * a, axis=-1, keepdims=True) + NORM_EPS)
        if ci < N_HEADS:
            a = a * (HEAD_DIM ** -0.5)
        qkv_ref[:, ci * HEAD_DIM:(ci + 1) * HEAD_DIM] = a

    beta_ref[...] = jax.nn.sigmoid(tail)
    g = -jnp.exp(a_log_ref[...]) * jax.nn.softplus(tail + dt_bias_ref[...])
    r_in_chunk = lax.broadcasted_iota(jnp.int32, (TS, TAIL), 0) % CHUNK
    gc = g
    k = 1
    while k < CHUNK:
        gc = gc + jnp.where(r_in_chunk >= k, pltpu.roll(gc, k, axis=0), 0.0)
        k *= 2
    gc_ref[...] = gc
    gct_ref[...] = gc.T

    row = lax.broadcasted_iota(jnp.int32, (CHUNK, CHUNK), 0)
    col = lax.broadcasted_iota(jnp.int32, (CHUNK, CHUNK), 1)
    eye = (row == col).astype(jnp.float32)
    masks = _level_masks(row, col, CHUNK)
    chunk_heads = [(c, h) for c in range(N_CHUNKS) for h in range(N_HEADS)]

    def head_cols(base, h):
        return slice(base + h * HEAD_DIM, base + (h + 1) * HEAD_DIM)

    def rows(c):
        return slice(c * CHUNK, (c + 1) * CHUNK)

    def gate_cols(c, h):
        bcol = beta_ref[rows(c), LANE_BETA + h:LANE_BETA + h + 1]
        gcol = gc_ref[rows(c), LANE_DECAY + h:LANE_DECAY + h + 1]
        glast = gc_ref[(c + 1) * CHUNK - 1:(c + 1) * CHUNK, LANE_DECAY + h:LANE_DECAY + h + 1]
        return bcol, gcol, glast

    a_mats, qks, t_mats = [], [], []
    for c, h in chunk_heads:
        bcol, gcol, _ = gate_cols(c, h)
        grow = gct_ref[LANE_DECAY + h:LANE_DECAY + h + 1, rows(c)]
        q = qkv_ref[rows(c), head_cols(0, h)]
        kk = qkv_ref[rows(c), head_cols(D_DN, h)]
        decay = jnp.exp(jnp.where(row >= col, gcol - grow, -jnp.inf))
        kq = _bdot_nt(jnp.concatenate([kk * bcol, q], axis=0), kk)
        a_mat = jnp.where(row > col, kq[:CHUNK] * decay, 0.0)
        a_mats.append(a_mat)
        qks.append((kq[CHUNK:] * decay).astype(jnp.bfloat16))
        t_mats.append(eye - jnp.where(masks[0], a_mat, 0.0))
    for m in masks[1:]:
        xs = [_bdot(t, jnp.where(m, a, 0.0)) for t, a in zip(t_mats, a_mats)]
        t_mats = [t - _bdot(x_, t) for t, x_ in zip(t_mats, xs)]
    uws = []
    for (c, h), t in zip(chunk_heads, t_mats):
        bcol, gcol, _ = gate_cols(c, h)
        kk = qkv_ref[rows(c), head_cols(D_DN, h)]
        v = qkv_ref[rows(c), head_cols(2 * D_DN, h)]
        rhs = jnp.concatenate([v * bcol, kk * (bcol * jnp.exp(gcol))], axis=1)
        uws.append(_bdot(t, rhs).astype(jnp.bfloat16))
    for (c, h), uw, qk in zip(chunk_heads, uws, qks):
        _, gcol, glast = gate_cols(c, h)
        q = qkv_ref[rows(c), head_cols(0, h)]
        kk = qkv_ref[rows(c), head_cols(D_DN, h)]
        nm = _bdot_tn(kk * jnp.exp(glast - gcol), uw)
        qw = jnp.dot(qk, uw, preferred_element_type=jnp.float32)
        n_ref[c, h] = nm[:, :HEAD_DIM]
        oi_ref[c, h] = qw[:, :HEAD_DIM]
        mq_ref[c, h] = jnp.concatenate(
            [nm[:, HEAD_DIM:], q * jnp.exp(gcol) - qw[:, HEAD_DIM:]], axis=0).astype(jnp.bfloat16)
        dec_ref[c, h] = jnp.broadcast_to(jnp.exp(glast), (8, HEAD_DIM))

    for c in range(N_CHUNKS):
        r0 = c * CHUNK
        for h in range(N_HEADS):
            state = s_ref[h]
            r = jnp.dot(mq_ref[c, h], state.astype(jnp.bfloat16), preferred_element_type=jnp.float32)
            s_ref[h] = state * dec_ref[c, h][0:1, :] + n_ref[c, h] - r[:HEAD_DIM]
            o = r[HEAD_DIM:] + oi_ref[c, h]
            o = o * lax.rsqrt(jnp.mean(o * o, axis=-1, keepdims=True) + NORM_EPS) * dn_norm_w_ref[...]
            dz = p_ref[HALO + r0:HALO + r0 + CHUNK, OFF_DZ + h * HEAD_DIM:OFF_DZ + (h + 1) * HEAD_DIM]
            y_ref[r0:r0 + CHUNK, D_POOL + h * HEAD_DIM:D_POOL + (h + 1) * HEAD_DIM] = o * _silu(dz)

    hres = x + jnp.dot(y_ref[...].astype(jnp.bfloat16), w_out_ref[...],
                       preferred_element_type=jnp.float32)
    out_ref[0] = (hres * lax.rsqrt(jnp.mean(hres * hres, axis=-1, keepdims=True) + NORM_EPS)
                  * fnorm_w_ref[...])


def _lane_row(vals, lane0):
    return jnp.zeros((1, TAIL), jnp.float32).at[0, lane0:lane0 + vals.shape[0]].set(vals)


def kernel(x, norm_w, w_in, pool_w, pool_scale, conv_w, a_log, dt_bias, dn_norm_w, w_out, final_norm_w):
    B, S, D = x.shape
    assert D == D_MODEL and S % TS == 0 and norm_w.shape[0] == 1
    w_in0 = w_in[0]
    w_main = w_in0[:, :D_MAIN].astype(jnp.bfloat16)
    w_tail = jnp.pad(w_in0[:, D_MAIN:], ((0, 0), (0, TAIL - 2 * N_HEADS))).astype(jnp.bfloat16)
    full = lambda shape: pl.BlockSpec(shape, lambda b, s: (0,) * len(shape))
    blk = pl.BlockSpec((1, TS, D_MODEL), lambda b, s: (b, s, 0))
    return pl.pallas_call(
        _block_kernel,
        grid=(B, S // TS),
        in_specs=[
            blk,
            full((1, D_MODEL)),
            full((D_MODEL, D_MAIN)),
            full((D_MODEL, TAIL)),
            full((len(POOL_WINDOWS), POOL_GROUP, POOL_GROUP)),
            full((1, D_POOL)),
            full((CONV_WIDTH, 3 * D_DN)),
            full((1, TAIL)),
            full((1, TAIL)),
            full((1, HEAD_DIM)),
            full((D_MODEL, D_MODEL)),
            full((1, D_MODEL)),
        ],
        out_specs=blk,
        out_shape=jax.ShapeDtypeStruct(x.shape, x.dtype),
        scratch_shapes=[
            pltpu.VMEM((HALO + TS, D_MAIN), jnp.float32),
            pltpu.VMEM((TS, 3 * D_DN), jnp.float32),
            pltpu.VMEM((TS, TAIL), jnp.float32),
            pltpu.VMEM((TAIL, TS), jnp.float32),
            pltpu.VMEM((TS, TAIL), jnp.float32),
            pltpu.VMEM((TS, D_MODEL), jnp.float32),
            pltpu.VMEM((N_HEADS, HEAD_DIM, HEAD_DIM), jnp.float32),
            pltpu.VMEM((N_CHUNKS, N_HEADS, HEAD_DIM + CHUNK, HEAD_DIM), jnp.bfloat16),
            pltpu.VMEM((N_CHUNKS, N_HEADS, HEAD_DIM, HEAD_DIM), jnp.float32),
            pltpu.VMEM((N_CHUNKS, N_HEADS, CHUNK, HEAD_DIM), jnp.float32),
            pltpu.VMEM((N_CHUNKS, N_HEADS, 8, HEAD_DIM), jnp.float32),
        ],
        compiler_params=pltpu.CompilerParams(
            dimension_semantics=("arbitrary", "arbitrary"),
            vmem_limit_bytes=VMEM_LIMIT_BYTES),
        name="hymba_block",
    )(x, norm_w, w_main, w_tail, pool_w[0].astype(jnp.bfloat16), pool_scale,
      conv_w[0], _lane_row(a_log[0], LANE_DECAY), _lane_row(dt_bias[0], LANE_DECAY),
      dn_norm_w, w_out[0].astype(jnp.bfloat16), final_norm_w[None, :])
```

```python
import jax
import jax.numpy as jnp
from jax import lax
from jax.experimental import pallas as pl
from jax.experimental.pallas import tpu as pltpu

D_MODEL = 1024
D_POOL = 512
D_DN = 512
POOL_WINDOWS = (2, 4, 8, 16)
POOL_GROUP = 128
HEAD_DIM = 128
N_HEADS = 4
CONV_WIDTH = 4
NORM_EPS = 1e-6

D_MAIN = 2 * D_POOL + 4 * D_DN
D_IN = D_MAIN + 2 * N_HEADS
TAIL = 128
OFF_PU, OFF_PZ, OFF_Q, OFF_K, OFF_V, OFF_DZ = 0, 512, 1024, 1536, 2048, 2560
LANE_BETA, LANE_DECAY = 0, N_HEADS

HALO = 16
CHUNK = 64
TS = 512
N_CHUNKS = TS // CHUNK
D1_GROUP_CHUNKS = 4
PROJ_COLS = 512
W_STAGE_ROWS = 128

VMEM_LIMIT_BYTES = 56 * 1024 * 1024


def _bdot(a, b):
    return jnp.dot(a.astype(jnp.bfloat16), b.astype(jnp.bfloat16),
                   preferred_element_type=jnp.float32)


def _bdot_nt(a, b):
    return lax.dot_general(a.astype(jnp.bfloat16), b.astype(jnp.bfloat16),
                           (((1,), (1,)), ((), ())), preferred_element_type=jnp.float32)


def _bdot_tn(a, b):
    return lax.dot_general(a.astype(jnp.bfloat16), b.astype(jnp.bfloat16),
                           (((0,), (0,)), ((), ())), preferred_element_type=jnp.float32)


def _silu(x):
    return x * jax.nn.sigmoid(x)


def _level_masks(row, col, n):
    masks = []
    r = 1
    while r < n:
        masks.append(((row // (2 * r)) == (col // (2 * r)))
                     & ((row % (2 * r)) >= r) & ((col % (2 * r)) < r))
        r *= 2
    return masks


def _delta_precompute(chunks, qkv_ref, gc_ref, gct_ref, beta_ref, mq_ref, n_ref, oi_ref, dec_ref,
                      row, col, eye, masks):
    chunk_heads = [(c, h) for c in chunks for h in range(N_HEADS)]

    def head_cols(base, h):
        return slice(base + h * HEAD_DIM, base + (h + 1) * HEAD_DIM)

    def rows(c):
        return slice(c * CHUNK, (c + 1) * CHUNK)

    def gate_cols(c, h):
        bcol = beta_ref[rows(c), LANE_BETA + h:LANE_BETA + h + 1]
        gcol = gc_ref[rows(c), LANE_DECAY + h:LANE_DECAY + h + 1]
        glast = gc_ref[(c + 1) * CHUNK - 1:(c + 1) * CHUNK, LANE_DECAY + h:LANE_DECAY + h + 1]
        return bcol, gcol, glast

    a_mats, qks, t_mats = [], [], []
    for c, h in chunk_heads:
        bcol, gcol, _ = gate_cols(c, h)
        grow = gct_ref[LANE_DECAY + h:LANE_DECAY + h + 1, rows(c)]
        q = qkv_ref[rows(c), head_cols(0, h)]
        kk = qkv_ref[rows(c), head_cols(D_DN, h)]
        decay = jnp.exp(jnp.where(row >= col, gcol - grow, -jnp.inf))
        kq = _bdot_nt(jnp.concatenate([kk * bcol, q], axis=0), kk)
        a_mat = jnp.where(row > col, kq[:CHUNK] * decay, 0.0)
        a_mats.append(a_mat)
        qks.append((kq[CHUNK:] * decay).astype(jnp.bfloat16))
        t_mats.append(eye - jnp.where(masks[0], a_mat, 0.0))
    for m in masks[1:]:
        xs = [_bdot(t, jnp.where(m, a, 0.0)) for t, a in zip(t_mats, a_mats)]
        t_mats = [t - _bdot(x_, t) for t, x_ in zip(t_mats, xs)]
    uws = []
    for (c, h), t in zip(chunk_heads, t_mats):
        bcol, gcol, _ = gate_cols(c, h)
        kk = qkv_ref[rows(c), head_cols(D_DN, h)]
        v = qkv_ref[rows(c), head_cols(2 * D_DN, h)]
        rhs = jnp.concatenate([v * bcol, kk * (bcol * jnp.exp(gcol))], axis=1)
        uws.append(_bdot(t, rhs).astype(jnp.bfloat16))
    for (c, h), uw, qk in zip(chunk_heads, uws, qks):
        _, gcol, glast = gate_cols(c, h)
        q = qkv_ref[rows(c), head_cols(0, h)]
        kk = qkv_ref[rows(c), head_cols(D_DN, h)]
        nm = _bdot_tn(kk * jnp.exp(glast - gcol), uw)
        qw = jnp.dot(qk, uw, preferred_element_type=jnp.float32)
        n_ref[c, h] = nm[:, :HEAD_DIM]
        oi_ref[c, h] = qw[:, :HEAD_DIM]
        mq_ref[c, h] = jnp.concatenate(
            [nm[:, HEAD_DIM:], q * jnp.exp(gcol) - qw[:, HEAD_DIM:]], axis=0).astype(jnp.bfloat16)
        dec_ref[c, h] = jnp.broadcast_to(jnp.exp(glast), (8, HEAD_DIM))


def _block_kernel(x_ref, norm_w_ref, w_in_ref, pool_w_ref, pool_scale_ref,
                  conv_w_ref, a_log_ref, dt_bias_ref, dn_norm_w_ref, w_out_ref, fnorm_w_ref,
                  out_ref,
                  w_main_ref, w_tail_ref, wo_ref, pw_ref,
                  p_ref, qkv_ref, gc_ref, gct_ref, beta_ref, y_ref, s_ref,
                  mq_ref, n_ref, oi_ref, dec_ref):
    sblk = pl.program_id(1)

    @pl.when((pl.program_id(0) == 0) & (sblk == 0))
    def _():
        w_tail_ref[...] = jnp.zeros_like(w_tail_ref)
        for r0 in range(0, D_MODEL, W_STAGE_ROWS):
            rs = slice(r0, r0 + W_STAGE_ROWS)
            w_main_ref[rs, :] = w_in_ref[0, rs, 0:D_MAIN].astype(jnp.bfloat16)
            w_tail_ref[rs, 0:2 * N_HEADS] = w_in_ref[0, rs, D_MAIN:D_IN].astype(jnp.bfloat16)
            wo_ref[rs, :] = w_out_ref[0, rs, :].astype(jnp.bfloat16)
        pw_ref[...] = pool_w_ref[0].astype(jnp.bfloat16)

    @pl.when(sblk == 0)
    def _():
        p_ref[0:HALO, :] = jnp.zeros((HALO, D_MAIN), jnp.float32)
        s_ref[...] = jnp.zeros_like(s_ref)

    @pl.when(sblk != 0)
    def _():
        p_ref[0:HALO, :] = p_ref[TS:TS + HALO, :]

    x = x_ref[0]
    n = x * lax.rsqrt(jnp.mean(x * x, axis=-1, keepdims=True) + NORM_EPS) * norm_w_ref[...]
    nb = n.astype(jnp.bfloat16)
    for c0 in range(0, D_MAIN, PROJ_COLS):
        p_ref[HALO:HALO + TS, c0:c0 + PROJ_COLS] = jnp.dot(
            nb, w_main_ref[:, c0:c0 + PROJ_COLS], preferred_element_type=jnp.float32)
    tail = jnp.dot(nb, w_tail_ref[...], preferred_element_type=jnp.float32)

    t_pos = sblk * TS + lax.broadcasted_iota(jnp.int32, (TS, 1), 0)
    for gi, w in enumerate(POOL_WINDOWS):
        c0 = OFF_PU + gi * POOL_GROUP
        e = p_ref[0:HALO + TS, c0:c0 + POOL_GROUP]
        s = e
        k = 1
        while k < w:
            s = s + pltpu.roll(s, k, axis=0)
            k *= 2
        u = e[HALO:]
        inv_cnt = 1.0 / jnp.minimum(t_pos + 1, w).astype(jnp.float32)
        mix = s[HALO:] * inv_cnt - u
        mixed = _bdot(mix, pw_ref[gi])
        z = p_ref[HALO:HALO + TS, OFF_PZ + gi * POOL_GROUP:OFF_PZ + (gi + 1) * POOL_GROUP]
        y_ref[:, c0:c0 + POOL_GROUP] = mixed * pool_scale_ref[:, c0:c0 + POOL_GROUP] * _silu(z)

    for ci in range(3 * D_DN // HEAD_DIM):
        c0 = OFF_Q + ci * HEAD_DIM
        cw = conv_w_ref[:, ci * HEAD_DIM:(ci + 1) * HEAD_DIM]
        e = p_ref[HALO - 8:HALO + TS, c0:c0 + HEAD_DIM]
        e2 = pltpu.roll(e, 2, axis=0)
        even = e * cw[3:4, :] + e2 * cw[1:2, :]
        odd = e * cw[2:3, :] + e2 * cw[0:1, :]
        acc = (even + pltpu.roll(odd, 1, axis=0))[8:]
        a = _silu(acc)
        if ci < 2 * N_HEADS:
            a = a * lax.rsqrt(jnp.sum(a * a, axis=-1, keepdims=True) + NORM_EPS)
        if ci < N_HEADS:
            a = a * (HEAD_DIM ** -0.5)
        qkv_ref[:, ci * HEAD_DIM:(ci + 1) * HEAD_DIM] = a

    beta_ref[...] = jax.nn.sigmoid(tail)
    g = -jnp.exp(a_log_ref[...]) * jax.nn.softplus(tail + dt_bias_ref[...])
    r_in_chunk = lax.broadcasted_iota(jnp.int32, (TS, TAIL), 0) % CHUNK
    gc = g
    k = 1
    while k < CHUNK:
        gc = gc + jnp.where(r_in_chunk >= k, pltpu.roll(gc, k, axis=0), 0.0)
        k *= 2
    gc_ref[...] = gc
    gct_ref[...] = gc.T

    row = lax.broadcasted_iota(jnp.int32, (CHUNK, CHUNK), 0)
    col = lax.broadcasted_iota(jnp.int32, (CHUNK, CHUNK), 1)
    eye = (row == col).astype(jnp.float32)
    masks = _level_masks(row, col, CHUNK)
    for c_first in range(0, N_CHUNKS, D1_GROUP_CHUNKS):
        _delta_precompute(range(c_first, c_first + D1_GROUP_CHUNKS), qkv_ref, gc_ref, gct_ref, beta_ref,
                          mq_ref, n_ref, oi_ref, dec_ref, row, col, eye, masks)

    for c in range(N_CHUNKS):
        r0 = c * CHUNK
        for h in range(N_HEADS):
            state = s_ref[h]
            r = jnp.dot(mq_ref[c, h], state.astype(jnp.bfloat16), preferred_element_type=jnp.float32)
            s_ref[h] = state * dec_ref[c, h][0:1, :] + n_ref[c, h] - r[:HEAD_DIM]
            o = r[HEAD_DIM:] + oi_ref[c, h]
            o = o * lax.rsqrt(jnp.mean(o * o, axis=-1, keepdims=True) + NORM_EPS) * dn_norm_w_ref[...]
            dz = p_ref[HALO + r0:HALO + r0 + CHUNK, OFF_DZ + h * HEAD_DIM:OFF_DZ + (h + 1) * HEAD_DIM]
            y_ref[r0:r0 + CHUNK, D_POOL + h * HEAD_DIM:D_POOL + (h + 1) * HEAD_DIM] = o * _silu(dz)

    hres = x + jnp.dot(y_ref[...].astype(jnp.bfloat16), wo_ref[...],
                       preferred_element_type=jnp.float32)
    out_ref[0] = (hres * lax.rsqrt(jnp.mean(hres * hres, axis=-1, keepdims=True) + NORM_EPS)
                  * fnorm_w_ref[...])


def _lane_row(vals, lane0):
    return jnp.zeros((1, TAIL), jnp.float32).at[0, lane0:lane0 + vals.shape[0]].set(vals)


def kernel(x, norm_w, w_in, pool_w, pool_scale, conv_w, a_log, dt_bias, dn_norm_w, w_out, final_norm_w):
    B, S, D = x.shape
    assert D == D_MODEL and S % TS == 0 and norm_w.shape[0] == 1 and w_in.shape == (1, D_MODEL, D_IN)

    def const(shape):
        return pl.BlockSpec(shape, lambda b, s: (0,) * len(shape), pipeline_mode=pl.Buffered(1))

    blk = pl.BlockSpec((1, TS, D_MODEL), lambda b, s: (b, s, 0))
    n_groups = len(POOL_WINDOWS)
    return pl.pallas_call(
        _block_kernel,
        grid=(B, S // TS),
        in_specs=[
            blk,
            const((1, D_MODEL)),
            const((1, D_MODEL, D_IN)),
            const((1, n_groups, POOL_GROUP, POOL_GROUP)),
            const((1, D_POOL)),
            const((CONV_WIDTH, 3 * D_DN)),
            const((1, TAIL)),
            const((1, TAIL)),
            const((1, HEAD_DIM)),
            const((1, D_MODEL, D_MODEL)),
            const((1, D_MODEL)),
        ],
        out_specs=blk,
        out_shape=jax.ShapeDtypeStruct(x.shape, x.dtype),
        scratch_shapes=[
            pltpu.VMEM((D_MODEL, D_MAIN), jnp.bfloat16),
            pltpu.VMEM((D_MODEL, TAIL), jnp.bfloat16),
            pltpu.VMEM((D_MODEL, D_MODEL), jnp.bfloat16),
            pltpu.VMEM((n_groups, POOL_GROUP, POOL_GROUP), jnp.bfloat16),
            pltpu.VMEM((HALO + TS, D_MAIN), jnp.float32),
            pltpu.VMEM((TS, 3 * D_DN), jnp.float32),
            pltpu.VMEM((TS, TAIL), jnp.float32),
            pltpu.VMEM((TAIL, TS), jnp.float32),
            pltpu.VMEM((TS, TAIL), jnp.float32),
            pltpu.VMEM((TS, D_MODEL), jnp.float32),
            pltpu.VMEM((N_HEADS, HEAD_DIM, HEAD_DIM), jnp.float32),
            pltpu.VMEM((N_CHUNKS, N_HEADS, HEAD_DIM + CHUNK, HEAD_DIM), jnp.bfloat16),
            pltpu.VMEM((N_CHUNKS, N_HEADS, HEAD_DIM, HEAD_DIM), jnp.float32),
            pltpu.VMEM((N_CHUNKS, N_HEADS, CHUNK, HEAD_DIM), jnp.float32),
            pltpu.VMEM((N_CHUNKS, N_HEADS, 8, HEAD_DIM), jnp.float32),
        ],
        compiler_params=pltpu.CompilerParams(
            dimension_semantics=("arbitrary", "arbitrary"),
            vmem_limit_bytes=VMEM_LIMIT_BYTES),
        name="hymba_block",
    )(x, norm_w, w_in, pool_w, pool_scale, conv_w[0],
      _lane_row(a_log[0], LANE_DECAY), _lane_row(dt_bias[0], LANE_DECAY),
      dn_norm_w, w_out, final_norm_w[None, :])
```

```python
import jax
import jax.numpy as jnp
from jax import lax
from jax.experimental import pallas as pl
from jax.experimental.pallas import tpu as pltpu

D_MODEL = 1024
D_POOL = 512
D_DN = 512
POOL_WINDOWS = (2, 4, 8, 16)
POOL_GROUP = 128
HEAD_DIM = 128
N_HEADS = 4
CONV_WIDTH = 4
NORM_EPS = 1e-6

D_MAIN = 2 * D_POOL + 4 * D_DN
D_IN = D_MAIN + 2 * N_HEADS
TAIL = 128
OFF_PU, OFF_PZ, OFF_Q, OFF_K, OFF_V, OFF_DZ = 0, 512, 1024, 1536, 2048, 2560
LANE_BETA, LANE_DECAY = 0, N_HEADS

HALO = 24
POOL_EXT = 16
N_SLABS = (2 * D_POOL + 4 * D_DN) // 128
GROUP_SLABS = 4
CHUNK = 64
TS = 512
N_CHUNKS = TS // CHUNK
D1_GROUP_CHUNKS = 8
PROJ_COLS = 512
W_STAGE_ROWS = 128

VMEM_LIMIT_BYTES = 56 * 1024 * 1024


def _bdot(a, b):
    return jnp.dot(a.astype(jnp.bfloat16), b.astype(jnp.bfloat16),
                   preferred_element_type=jnp.float32)


def _bdot_nt(a, b):
    return lax.dot_general(a.astype(jnp.bfloat16), b.astype(jnp.bfloat16),
                           (((1,), (1,)), ((), ())), preferred_element_type=jnp.float32)


def _bdot_tn(a, b):
    return lax.dot_general(a.astype(jnp.bfloat16), b.astype(jnp.bfloat16),
                           (((0,), (0,)), ((), ())), preferred_element_type=jnp.float32)


def _silu(x):
    h = 0.5 * x
    return h + h * jnp.tanh(h)


def _packed_constants():
    shape = (CHUNK, N_HEADS * CHUNK)
    row = lax.broadcasted_iota(jnp.int32, shape, 0)
    lane = lax.broadcasted_iota(jnp.int32, shape, 1)
    col = lane % CHUNK
    blk = lane // CHUNK
    f32 = jnp.float32
    levels = []
    r = 1
    while r < CHUNK:
        levels.append((((row // (2 * r)) == (col // (2 * r)))
                       & ((row % (2 * r)) >= r) & ((col % (2 * r)) < r)).astype(f32))
        r *= 2
    return dict(
        causal=row >= col,
        strict=(row > col).astype(f32),
        eye=(row == col).astype(f32),
        levels=levels,
        blk=blk,
        head=[(blk == h).astype(jnp.bfloat16) for h in range(N_HEADS)],
    )


def _block_diag(p, consts):
    pb = p.astype(jnp.bfloat16)
    return jnp.concatenate([pb * m for m in consts["head"]], axis=0)


def _delta_precompute(chunks, qkv_ref, gate_ref, gatet_ref, mq_ref, n_ref, oi_ref, dec_ref, consts):
    f32, bf16 = jnp.float32, jnp.bfloat16
    heads = range(N_HEADS)

    def rows(c):
        return slice(c * CHUNK, (c + 1) * CHUNK)

    def lane_row(c, first):
        return jnp.concatenate([gatet_ref[first + h:first + h + 1, rows(c)] for h in heads], axis=1)

    def stack(parts):
        return jnp.concatenate(parts, axis=0)

    a_mats, t_mats, qkbs, egrows = [], [], [], []
    for c in chunks:
        brow = lane_row(c, LANE_BETA)
        grow = lane_row(c, LANE_DECAY)
        gcol = None
        for h in reversed(heads):
            bc = jnp.broadcast_to(gate_ref[rows(c), LANE_DECAY + h:LANE_DECAY + h + 1],
                                  (CHUNK, N_HEADS * CHUNK))
            gcol = bc if gcol is None else jnp.where(consts["blk"] == h, bc, gcol)
        db = jnp.exp(jnp.where(consts["causal"], gcol - grow, -jnp.inf)) * brow
        k16 = [qkv_ref[N_HEADS + h, rows(c), :].astype(bf16) for h in heads]
        q16 = [qkv_ref[h, rows(c), :].astype(bf16) for h in heads]
        zero = jnp.zeros((CHUNK, HEAD_DIM), bf16)
        kdiag = jnp.concatenate(
            [jnp.concatenate([k16[h] if j == h else zero for j in heads], axis=1) for h in heads], axis=0)
        lhs = jnp.concatenate([jnp.concatenate(k16, axis=1), jnp.concatenate(q16, axis=1)], axis=0)
        kq = lax.dot_general(lhs, kdiag, (((1,), (1,)), ((), ())), preferred_element_type=f32)
        a_mat = kq[:CHUNK] * db * consts["strict"]
        a_mats.append(a_mat)
        qkbs.append(kq[CHUNK:] * db)
        t_mats.append(consts["eye"] - a_mat * consts["levels"][0])
        egrows.append(jnp.exp(grow))
    yield
    a_diags = [_block_diag(a, consts) for a in a_mats]
    for m in consts["levels"][1:]:
        ys = [jnp.dot(t.astype(bf16), ad, preferred_element_type=f32) for t, ad in zip(t_mats, a_diags)]
        t_mats = [t - jnp.dot(y.astype(bf16), _block_diag(t, consts), preferred_element_type=f32) * m
                  for t, y in zip(t_mats, ys)]
        yield
    uws = []
    for c, t, eg in zip(chunks, t_mats, egrows):
        v_stack = stack([qkv_ref[2 * N_HEADS + h, rows(c), :] for h in heads]).astype(bf16)
        k_stack = stack([qkv_ref[N_HEADS + h, rows(c), :] for h in heads]).astype(bf16)
        u = jnp.dot(_block_diag(t, consts), v_stack, preferred_element_type=f32)
        w = jnp.dot(_block_diag(t * eg, consts), k_stack, preferred_element_type=f32)
        uws.append(jnp.concatenate([u, w], axis=1).astype(bf16))
    yield
    for c, uw, qkb in zip(chunks, uws, qkbs):
        qw = jnp.dot(_block_diag(qkb, consts), uw, preferred_element_type=f32)
        for h in heads:
            hs = slice(h * CHUNK, (h + 1) * CHUNK)
            bcol = gate_ref[rows(c), LANE_BETA + h:LANE_BETA + h + 1]
            gcol = gate_ref[rows(c), LANE_DECAY + h:LANE_DECAY + h + 1]
            glast = gate_ref[(c + 1) * CHUNK - 1:(c + 1) * CHUNK, LANE_DECAY + h:LANE_DECAY + h + 1]
            kdb = qkv_ref[N_HEADS + h, rows(c), :] * (jnp.exp(glast - gcol) * bcol)
            nm = _bdot_tn(kdb, uw[hs])
            n_ref[c, h] = nm[:, :HEAD_DIM]
            oi_ref[c, h] = qw[hs, :HEAD_DIM]
            q_eff = qkv_ref[h, rows(c), :] * jnp.exp(gcol) - qw[hs, HEAD_DIM:]
            mq_ref[c, h] = jnp.concatenate([nm[:, HEAD_DIM:], q_eff], axis=0).astype(bf16)
            dec_ref[c, h] = jnp.broadcast_to(jnp.exp(glast), (8, HEAD_DIM))


def _interleave(stages, tasks):
    for _ in stages:
        if tasks:
            tasks.pop(0)()
    while tasks:
        tasks.pop(0)()


def _block_kernel(x_ref, norm_w_ref, w_in_ref, pool_w_ref, pool_scale_ref,
                  conv_w_ref, a_log_ref, dt_bias_ref, dn_norm_w_ref, w_out_ref, fnorm_w_ref,
                  out_ref,
                  w_main_ref, w_tail_ref, wo_ref, pw_ref,
                  pu_ref, pz_ref, pq_ref, pk_ref, pv_ref, pdz_ref,
                  ps_ref, qkv_ref, gate_ref, gatet_ref, y_ref, s_ref,
                  mq_ref, n_ref, oi_ref, dec_ref):
    sblk = pl.program_id(1)
    group_refs = (pu_ref, pz_ref, pq_ref, pk_ref, pv_ref, pdz_ref)

    def pslab(slab):
        return group_refs[slab // GROUP_SLABS], slab % GROUP_SLABS

    @pl.when((pl.program_id(0) == 0) & (sblk == 0))
    def _():
        w_tail_ref[...] = jnp.zeros_like(w_tail_ref)
        for r0 in range(0, D_MODEL, W_STAGE_ROWS):
            rs = slice(r0, r0 + W_STAGE_ROWS)
            w_main_ref[rs, :] = w_in_ref[0, rs, 0:D_MAIN].astype(jnp.bfloat16)
            w_tail_ref[rs, 0:2 * N_HEADS] = w_in_ref[0, rs, D_MAIN:D_IN].astype(jnp.bfloat16)
            wo_ref[rs, :] = w_out_ref[0, rs, :].astype(jnp.bfloat16)
        pw_ref[...] = pool_w_ref[0].astype(jnp.bfloat16)
        ps_ref[:, 0:HALO - POOL_EXT, :] = jnp.zeros((2, HALO - POOL_EXT, 128), jnp.float32)

    @pl.when(sblk == 0)
    def _():
        for ref in group_refs:
            ref[:, 0:HALO, :] = jnp.zeros((GROUP_SLABS, HALO, 128), jnp.float32)
        s_ref[...] = jnp.zeros_like(s_ref)

    @pl.when(sblk != 0)
    def _():
        for ref in group_refs:
            ref[:, 0:HALO, :] = ref[:, TS:TS + HALO, :]

    x = x_ref[0]
    n = x * lax.rsqrt(jnp.mean(x * x, axis=-1, keepdims=True) + NORM_EPS) * norm_w_ref[...]
    nb = n.astype(jnp.bfloat16)
    def proj_pair(slab):
        def run():
            c0 = slab * 128
            res = jnp.dot(nb, w_main_ref[:, c0:c0 + 256], preferred_element_type=jnp.float32)
            ref, j = pslab(slab)
            ref[j, HALO:HALO + TS, :] = res[:, :128]
            ref[j + 1, HALO:HALO + TS, :] = res[:, 128:]
        return run

    def taps(ref, slab, first_row, n_rows):
        return ref[slab, pl.ds(first_row, n_rows, stride=2), :]

    t_pos = sblk * TS + lax.broadcasted_iota(jnp.int32, (TS, 1), 0)

    def pool_group(gi, w):
        def run():
            slab = OFF_PU // 128 + gi
            n_half = (POOL_EXT + TS) // 2
            src, src_slab = pslab(slab)
            k = 1
            level = 0
            while k < w:
                for par in range(2):
                    first = HALO - POOL_EXT + par
                    ssum = taps(src, src_slab, first, n_half) + taps(src, src_slab, first - k, n_half)
                    ps_ref[level % 2, pl.ds(first, n_half, stride=2), :] = ssum
                src, src_slab = ps_ref, level % 2
                k *= 2
                level += 1
            ssum = src[src_slab, HALO:HALO + TS, :]
            u = pu_ref[gi, HALO:HALO + TS, :]
            inv_cnt = 1.0 / jnp.minimum(t_pos + 1, w).astype(jnp.float32)
            mix = ssum * inv_cnt - u
            mixed = _bdot(mix, pw_ref[gi])
            z = pz_ref[gi, HALO:HALO + TS, :]
            c0 = OFF_PU + gi * POOL_GROUP
            y_ref[:, c0:c0 + POOL_GROUP] = (
                mixed * pool_scale_ref[:, c0:c0 + POOL_GROUP] * _silu(z)).astype(jnp.bfloat16)
        return run

    def conv_group(ci):
        def run():
            slab = OFF_Q // 128 + ci
            cw = conv_w_ref[:, ci * HEAD_DIM:(ci + 1) * HEAD_DIM]
            for par in range(2):
                acc = None
                for j in range(CONV_WIDTH):
                    term = taps(*pslab(slab), HALO + par - (CONV_WIDTH - 1) + j, TS // 2) * cw[j:j + 1, :]
                    acc = term if acc is None else acc + term
                a = _silu(acc)
                if ci < 2 * N_HEADS:
                    a = a * lax.rsqrt(jnp.sum(a * a, axis=-1, keepdims=True) + NORM_EPS)
                if ci < N_HEADS:
                    a = a * (HEAD_DIM ** -0.5)
                qkv_ref[ci, pl.ds(par, TS // 2, stride=2), :] = a
        return run

    convs = [conv_group(ci) for ci in range(3 * N_HEADS)]
    pools = [pool_group(gi, w) for gi, w in enumerate(POOL_WINDOWS)]
    first_slab = {name: off // 128 for name, off in
                  dict(q=OFF_Q, k=OFF_K, v=OFF_V, pu=OFF_PU, pz=OFF_PZ, dz=OFF_DZ).items()}
    pair_order = [first_slab[name] + j for name in ("q", "k", "v", "pu", "pz", "dz") for j in (0, 2)]
    mixers = [None] * N_HEADS + convs + [None] * N_HEADS + pools
    for i, slab in enumerate(pair_order):
        proj_pair(slab)()
        for task in mixers[2 * i:2 * i + 2]:
            if task is not None:
                task()
    tail = jnp.dot(nb, w_tail_ref[...], preferred_element_type=jnp.float32)

    gdec = -jnp.exp(a_log_ref[...]) * jax.nn.softplus(tail + dt_bias_ref[...])
    r_in_chunk = lax.broadcasted_iota(jnp.int32, (TS, TAIL), 0) % CHUNK
    gc = gdec
    k = 1
    while k < CHUNK:
        gc = gc + jnp.where(r_in_chunk >= k, pltpu.roll(gc, k, axis=0), 0.0)
        k *= 2
    lane_t = lax.broadcasted_iota(jnp.int32, (TS, TAIL), 1)
    gates = jnp.where(lane_t < LANE_DECAY, jax.nn.sigmoid(tail), gc)
    gate_ref[...] = gates
    gatet_ref[...] = gates.T

    consts = _packed_constants()

    def recurrence_chunk(c):
        def run():
            r0 = c * CHUNK
            for h in range(N_HEADS):
                state = s_ref[h]
                r = jnp.dot(mq_ref[c, h], state.astype(jnp.bfloat16), preferred_element_type=jnp.float32)
                s_ref[h] = state * dec_ref[c, h][0:1, :] + n_ref[c, h] - r[:HEAD_DIM]
                o = r[HEAD_DIM:] + oi_ref[c, h]
                o = o * lax.rsqrt(jnp.mean(o * o, axis=-1, keepdims=True) + NORM_EPS) * dn_norm_w_ref[...]
                dz = pdz_ref[h, HALO + r0:HALO + r0 + CHUNK, :]
                y_ref[r0:r0 + CHUNK, D_POOL + h * HEAD_DIM:D_POOL + (h + 1) * HEAD_DIM] = (
                    o * _silu(dz)).astype(jnp.bfloat16)
        return run

    def output_rows(r0, r1):
        def run():
            hres = x_ref[0, r0:r1, :] + jnp.dot(y_ref[r0:r1, :], wo_ref[...],
                                                 preferred_element_type=jnp.float32)
            out_ref[0, r0:r1, :] = (
                hres * lax.rsqrt(jnp.mean(hres * hres, axis=-1, keepdims=True) + NORM_EPS)
                * fnorm_w_ref[...])
        return run

    pending, done_rows = [], None
    for c_first in range(0, N_CHUNKS, D1_GROUP_CHUNKS):
        chunks = range(c_first, c_first + D1_GROUP_CHUNKS)
        _interleave(_delta_precompute(chunks, qkv_ref, gate_ref, gatet_ref,
                                      mq_ref, n_ref, oi_ref, dec_ref, consts), pending)
        pending = [recurrence_chunk(c) for c in chunks]
        if done_rows is not None:
            pending.insert(1, output_rows(*done_rows))
        done_rows = (c_first * CHUNK, (c_first + D1_GROUP_CHUNKS) * CHUNK)
    for task in pending:
        task()
    output_rows(*done_rows)()


def _lane_row(vals, lane0):
    return jnp.zeros((1, TAIL), jnp.float32).at[0, lane0:lane0 + vals.shape[0]].set(vals)


def kernel(x, norm_w, w_in, pool_w, pool_scale, conv_w, a_log, dt_bias, dn_norm_w, w_out, final_norm_w):
    B, S, D = x.shape
    assert D == D_MODEL and S % TS == 0 and norm_w.shape[0] == 1 and w_in.shape == (1, D_MODEL, D_IN)

    def const(shape):
        return pl.BlockSpec(shape, lambda b, s: (0,) * len(shape), pipeline_mode=pl.Buffered(1))

    blk = pl.BlockSpec((1, TS, D_MODEL), lambda b, s: (b, s, 0))
    n_groups = len(POOL_WINDOWS)
    return pl.pallas_call(
        _block_kernel,
        grid=(B, S // TS),
        in_specs=[
            blk,
            const((1, D_MODEL)),
            const((1, D_MODEL, D_IN)),
            const((1, n_groups, POOL_GROUP, POOL_GROUP)),
            const((1, D_POOL)),
            const((CONV_WIDTH, 3 * D_DN)),
            const((1, TAIL)),
            const((1, TAIL)),
            const((1, HEAD_DIM)),
            const((1, D_MODEL, D_MODEL)),
            const((1, D_MODEL)),
        ],
        out_specs=blk,
        out_shape=jax.ShapeDtypeStruct(x.shape, x.dtype),
        scratch_shapes=[
            pltpu.VMEM((D_MODEL, D_MAIN), jnp.bfloat16),
            pltpu.VMEM((D_MODEL, TAIL), jnp.bfloat16),
            pltpu.VMEM((D_MODEL, D_MODEL), jnp.bfloat16),
            pltpu.VMEM((n_groups, POOL_GROUP, POOL_GROUP), jnp.bfloat16),
        ] + [pltpu.VMEM((GROUP_SLABS, HALO + TS, 128), jnp.float32)
             for _ in range(N_SLABS // GROUP_SLABS)] + [
            pltpu.VMEM((2, HALO + TS, 128), jnp.float32),
            pltpu.VMEM((3 * N_HEADS, TS, HEAD_DIM), jnp.float32),
            pltpu.VMEM((TS, TAIL), jnp.float32),
            pltpu.VMEM((TAIL, TS), jnp.float32),
            pltpu.VMEM((TS, D_MODEL), jnp.bfloat16),
            pltpu.VMEM((N_HEADS, HEAD_DIM, HEAD_DIM), jnp.float32),
            pltpu.VMEM((N_CHUNKS, N_HEADS, HEAD_DIM + CHUNK, HEAD_DIM), jnp.bfloat16),
            pltpu.VMEM((N_CHUNKS, N_HEADS, HEAD_DIM, HEAD_DIM), jnp.float32),
            pltpu.VMEM((N_CHUNKS, N_HEADS, CHUNK, HEAD_DIM), jnp.float32),
            pltpu.VMEM((N_CHUNKS, N_HEADS, 8, HEAD_DIM), jnp.float32),
        ],
        compiler_params=pltpu.CompilerParams(
            dimension_semantics=("arbitrary", "arbitrary"),
            vmem_limit_bytes=VMEM_LIMIT_BYTES),
        name="hymba_block",
    )(x, norm_w, w_in, pool_w, pool_scale, conv_w[0],
      _lane_row(a_log[0], LANE_DECAY), _lane_row(dt_bias[0], LANE_DECAY),
      dn_norm_w, w_out, final_norm_w[None, :])
```

```python
import jax
import jax.numpy as jnp
from jax import lax
from jax.experimental import pallas as pl
from jax.experimental.pallas import tpu as pltpu

D_MODEL = 1024
D_POOL = 512
D_DN = 512
POOL_WINDOWS = (2, 4, 8, 16)
POOL_GROUP = 128
HEAD_DIM = 128
N_HEADS = 4
CONV_WIDTH = 4
NORM_EPS = 1e-6

D_MAIN = 2 * D_POOL + 4 * D_DN
D_IN = D_MAIN + 2 * N_HEADS
TAIL = 128
OFF_PU, OFF_PZ, OFF_Q, OFF_K, OFF_V, OFF_DZ = 0, 512, 1024, 1536, 2048, 2560
LANE_BETA, LANE_DECAY = 0, N_HEADS

HALO = 24
POOL_EXT = 16
N_SLABS = (2 * D_POOL + 4 * D_DN) // 128
GROUP_SLABS = 4
CHUNK = 64
TS = 512
N_CHUNKS = TS // CHUNK
PROJ_COLS = 512
W_STAGE_ROWS = 128

VMEM_LIMIT_BYTES = 56 * 1024 * 1024


def _bdot(a, b):
    return jnp.dot(a.astype(jnp.bfloat16), b.astype(jnp.bfloat16),
                   preferred_element_type=jnp.float32)


def _bdot_nt(a, b):
    return lax.dot_general(a.astype(jnp.bfloat16), b.astype(jnp.bfloat16),
                           (((1,), (1,)), ((), ())), preferred_element_type=jnp.float32)


def _bdot_tn(a, b):
    return lax.dot_general(a.astype(jnp.bfloat16), b.astype(jnp.bfloat16),
                           (((0,), (0,)), ((), ())), preferred_element_type=jnp.float32)


def _silu(x):
    h = 0.5 * x
    return h + h * jnp.tanh(h)


def _packed_constants():
    shape = (CHUNK, N_HEADS * CHUNK)
    row = lax.broadcasted_iota(jnp.int32, shape, 0)
    lane = lax.broadcasted_iota(jnp.int32, shape, 1)
    col = lane % CHUNK
    blk = lane // CHUNK
    f32 = jnp.float32
    levels = []
    r = 1
    while r < CHUNK:
        levels.append((((row // (2 * r)) == (col // (2 * r)))
                       & ((row % (2 * r)) >= r) & ((col % (2 * r)) < r)).astype(f32))
        r *= 2
    return dict(
        causal=row >= col,
        strict=(row > col).astype(f32),
        eye=(row == col).astype(f32),
        levels=levels,
        blk=blk,
        head=[(blk == h).astype(jnp.bfloat16) for h in range(N_HEADS)],
    )


def _block_diag(p, consts):
    pb = p.astype(jnp.bfloat16)
    return jnp.concatenate([pb * m for m in consts["head"]], axis=0)


def _delta_precompute(chunks, qkv_ref, gate_ref, gatet_ref, mq_ref, n_ref, oi_ref, dec_ref, consts):
    f32, bf16 = jnp.float32, jnp.bfloat16
    heads = range(N_HEADS)

    def rows(c):
        return slice(c * CHUNK, (c + 1) * CHUNK)

    def lane_row(c, first):
        return jnp.concatenate([gatet_ref[first + h:first + h + 1, rows(c)] for h in heads], axis=1)

    def stack(parts):
        return jnp.concatenate(parts, axis=0)

    a_mats, t_mats, qkbs, egrows = [], [], [], []
    for c in chunks:
        brow = lane_row(c, LANE_BETA)
        grow = lane_row(c, LANE_DECAY)
        gcol = None
        for h in reversed(heads):
            bc = jnp.broadcast_to(gate_ref[rows(c), LANE_DECAY + h:LANE_DECAY + h + 1],
                                  (CHUNK, N_HEADS * CHUNK))
            gcol = bc if gcol is None else jnp.where(consts["blk"] == h, bc, gcol)
        db = jnp.exp(jnp.where(consts["causal"], gcol - grow, -jnp.inf)) * brow
        k16 = [qkv_ref[N_HEADS + h, rows(c), :].astype(bf16) for h in heads]
        q16 = [qkv_ref[h, rows(c), :].astype(bf16) for h in heads]
        zero = jnp.zeros((CHUNK, HEAD_DIM), bf16)
        kdiag = jnp.concatenate(
            [jnp.concatenate([k16[h] if j == h else zero for j in heads], axis=1) for h in heads], axis=0)
        lhs = jnp.concatenate([jnp.concatenate(k16, axis=1), jnp.concatenate(q16, axis=1)], axis=0)
        kq = lax.dot_general(lhs, kdiag, (((1,), (1,)), ((), ())), preferred_element_type=f32)
        a_mat = kq[:CHUNK] * db * consts["strict"]
        a_mats.append(a_mat)
        qkbs.append(kq[CHUNK:] * db)
        t_mats.append(consts["eye"] - a_mat * consts["levels"][0])
        egrows.append(jnp.exp(grow))
    g_mats = [a - jnp.dot(a.astype(bf16), _block_diag(a * consts["levels"][0], consts),
                          preferred_element_type=f32) for a in a_mats]
    for m in consts["levels"][1:]:
        tgs = [jnp.dot(jnp.concatenate([t, g], axis=0).astype(bf16), _block_diag(g * m, consts),
                       preferred_element_type=f32) for t, g in zip(t_mats, g_mats)]
        t_mats = [t - tg[:CHUNK] for t, tg in zip(t_mats, tgs)]
        g_mats = [g - tg[CHUNK:] for g, tg in zip(g_mats, tgs)]
    uws = []
    for c, t, eg in zip(chunks, t_mats, egrows):
        v_stack = stack([qkv_ref[2 * N_HEADS + h, rows(c), :] for h in heads]).astype(bf16)
        k_stack = stack([qkv_ref[N_HEADS + h, rows(c), :] for h in heads]).astype(bf16)
        u = jnp.dot(_block_diag(t, consts), v_stack, preferred_element_type=f32)
        w = jnp.dot(_block_diag(t * eg, consts), k_stack, preferred_element_type=f32)
        uws.append(jnp.concatenate([u, w], axis=1).astype(bf16))
    for c, uw, qkb in zip(chunks, uws, qkbs):
        qw = jnp.dot(_block_diag(qkb, consts), uw, preferred_element_type=f32)
        for h in heads:
            hs = slice(h * CHUNK, (h + 1) * CHUNK)
            bcol = gate_ref[rows(c), LANE_BETA + h:LANE_BETA + h + 1]
            gcol = gate_ref[rows(c), LANE_DECAY + h:LANE_DECAY + h + 1]
            glast = gate_ref[(c + 1) * CHUNK - 1:(c + 1) * CHUNK, LANE_DECAY + h:LANE_DECAY + h + 1]
            kdb = qkv_ref[N_HEADS + h, rows(c), :] * (jnp.exp(glast - gcol) * bcol)
            nm = _bdot_tn(kdb, uw[hs])
            n_ref[c, h] = nm[:, :HEAD_DIM]
            oi_ref[c, h] = qw[hs, :HEAD_DIM]
            q_eff = qkv_ref[h, rows(c), :] * jnp.exp(gcol) - qw[hs, HEAD_DIM:]
            mq_ref[c, h] = jnp.concatenate([nm[:, HEAD_DIM:], q_eff], axis=0).astype(bf16)
            dec_ref[c, h] = jnp.broadcast_to(jnp.exp(glast), (8, HEAD_DIM))


def _block_kernel(x_ref, norm_w_ref, w_in_ref, pool_w_ref, pool_scale_ref,
                  conv_w_ref, a_log_ref, dt_bias_ref, dn_norm_w_ref, w_out_ref, fnorm_w_ref,
                  out_ref,
                  w_main_ref, w_tail_ref, wo_ref, pw_ref,
                  pu_ref, pz_ref, pq_ref, pk_ref, pv_ref, pdz_ref,
                  ps_ref, qkv_ref, gate_ref, gatet_ref, y_ref, s_ref,
                  mq_ref, n_ref, oi_ref, dec_ref):
    sblk = pl.program_id(1)
    first_step = (pl.program_id(0) == 0) & (sblk == 0)
    group_refs = (pu_ref, pz_ref, pq_ref, pk_ref, pv_ref, pdz_ref)

    def pslab(slab):
        return group_refs[slab // GROUP_SLABS], slab % GROUP_SLABS

    @pl.when(first_step)
    def _():
        w_tail_ref[...] = jnp.zeros_like(w_tail_ref)
        for r0 in range(0, D_MODEL, W_STAGE_ROWS):
            rs = slice(r0, r0 + W_STAGE_ROWS)
            w_main_ref[rs, :] = w_in_ref[0, rs, 0:D_MAIN].astype(jnp.bfloat16)
            w_tail_ref[rs, 0:2 * N_HEADS] = w_in_ref[0, rs, D_MAIN:D_IN].astype(jnp.bfloat16)
            wo_ref[rs, :] = w_out_ref[0, rs, :].astype(jnp.bfloat16)
        pw_ref[...] = pool_w_ref[0].astype(jnp.bfloat16)
        ps_ref[:, 0:HALO - POOL_EXT, :] = jnp.zeros((2, HALO - POOL_EXT, 128), jnp.float32)

    @pl.when(sblk == 0)
    def _():
        for ref in group_refs:
            ref[:, 0:HALO, :] = jnp.zeros((GROUP_SLABS, HALO, 128), jnp.float32)
        s_ref[...] = jnp.zeros_like(s_ref)

    @pl.when(sblk != 0)
    def _():
        for ref in group_refs:
            ref[:, 0:HALO, :] = ref[:, TS:TS + HALO, :]

    x = x_ref[0]
    n = x * lax.rsqrt(jnp.mean(x * x, axis=-1, keepdims=True) + NORM_EPS) * norm_w_ref[...]
    nb = n.astype(jnp.bfloat16)

    tail = jnp.dot(nb, w_tail_ref[...], preferred_element_type=jnp.float32)

    gdec = -jnp.exp(a_log_ref[...]) * jax.nn.softplus(tail + dt_bias_ref[...])
    r_in_chunk = lax.broadcasted_iota(jnp.int32, (TS, TAIL), 0) % CHUNK
    gc = gdec
    k = 1
    while k < CHUNK:
        gc = gc + jnp.where(r_in_chunk >= k, pltpu.roll(gc, k, axis=0), 0.0)
        k *= 2
    lane_t = lax.broadcasted_iota(jnp.int32, (TS, TAIL), 1)
    gates = jnp.where(lane_t < LANE_DECAY, jax.nn.sigmoid(tail), gc)
    gate_ref[...] = gates
    gatet_ref[...] = gates.T

    def proj_pair(slab):
        def run():
            c0 = slab * 128
            res = jnp.dot(nb, w_main_ref[:, c0:c0 + 256], preferred_element_type=jnp.float32)
            ref, j = pslab(slab)
            ref[j, HALO:HALO + TS, :] = res[:, :128]
            ref[j + 1, HALO:HALO + TS, :] = res[:, 128:]
        return run

    def taps(ref, slab, first_row, n_rows):
        return ref[slab, pl.ds(first_row, n_rows, stride=2), :]

    t_pos = sblk * TS + lax.broadcasted_iota(jnp.int32, (TS, 1), 0)

    def pool_group(gi, w):
        def run():
            slab = OFF_PU // 128 + gi
            n_half = (POOL_EXT + TS) // 2
            src, src_slab = pslab(slab)
            k = 1
            level = 0
            while k < w:
                for par in range(2):
                    first = HALO - POOL_EXT + par
                    ssum = taps(src, src_slab, first, n_half) + taps(src, src_slab, first - k, n_half)
                    ps_ref[level % 2, pl.ds(first, n_half, stride=2), :] = ssum
                src, src_slab = ps_ref, level % 2
                k *= 2
                level += 1
            ssum = src[src_slab, HALO:HALO + TS, :]
            u = pu_ref[gi, HALO:HALO + TS, :]
            inv_cnt = 1.0 / jnp.minimum(t_pos + 1, w).astype(jnp.float32)
            mix = ssum * inv_cnt - u
            mixed = _bdot(mix, pw_ref[gi])
            z = pz_ref[gi, HALO:HALO + TS, :]
            c0 = OFF_PU + gi * POOL_GROUP
            y_ref[:, c0:c0 + POOL_GROUP] = (
                mixed * pool_scale_ref[:, c0:c0 + POOL_GROUP] * _silu(z)).astype(jnp.bfloat16)
        return run

    def conv_group(ci):
        def run():
            slab = OFF_Q // 128 + ci
            cw = conv_w_ref[:, ci * HEAD_DIM:(ci + 1) * HEAD_DIM]
            for par in range(2):
                acc = None
                for j in range(CONV_WIDTH):
                    term = taps(*pslab(slab), HALO + par - (CONV_WIDTH - 1) + j, TS // 2) * cw[j:j + 1, :]
                    acc = term if acc is None else acc + term
                a = _silu(acc)
                if ci < 2 * N_HEADS:
                    a = a * lax.rsqrt(jnp.sum(a * a, axis=-1, keepdims=True) + NORM_EPS)
                if ci < N_HEADS:
                    a = a * (HEAD_DIM ** -0.5)
                qkv_ref[ci, pl.ds(par, TS // 2, stride=2), :] = a
        return run

    convs = [conv_group(ci) for ci in range(3 * N_HEADS)]
    pools = [pool_group(gi, w) for gi, w in enumerate(POOL_WINDOWS)]
    first_slab = {name: off // 128 for name, off in
                  dict(q=OFF_Q, k=OFF_K, v=OFF_V, pu=OFF_PU, pz=OFF_PZ, dz=OFF_DZ).items()}
    pair_order = [first_slab[name] + j for name in ("q", "k", "v", "pu", "pz", "dz") for j in (0, 2)]
    mixers = [None] * N_HEADS + convs + [None] * N_HEADS + pools
    for i, slab in enumerate(pair_order):
        proj_pair(slab)()
        for task in mixers[2 * i:2 * i + 2]:
            if task is not None:
                task()
    consts = _packed_constants()

    def recurrence_chunk(c):
        def run():
            r0 = c * CHUNK
            for h in range(N_HEADS):
                state = s_ref[h]
                r = jnp.dot(mq_ref[c, h], state.astype(jnp.bfloat16), preferred_element_type=jnp.float32)
                s_ref[h] = state * dec_ref[c, h][0:1, :] + n_ref[c, h] - r[:HEAD_DIM]
                o = r[HEAD_DIM:] + oi_ref[c, h]
                o = o * lax.rsqrt(jnp.mean(o * o, axis=-1, keepdims=True) + NORM_EPS) * dn_norm_w_ref[...]
                dz = pdz_ref[h, HALO + r0:HALO + r0 + CHUNK, :]
                y_ref[r0:r0 + CHUNK, D_POOL + h * HEAD_DIM:D_POOL + (h + 1) * HEAD_DIM] = (
                    o * _silu(dz)).astype(jnp.bfloat16)
        return run

    def output_rows(r0, r1):
        def run():
            hres = x_ref[0, r0:r1, :] + jnp.dot(y_ref[r0:r1, :], wo_ref[...],
                                                 preferred_element_type=jnp.float32)
            out_ref[0, r0:r1, :] = (
                hres * lax.rsqrt(jnp.mean(hres * hres, axis=-1, keepdims=True) + NORM_EPS)
                * fnorm_w_ref[...])
        return run

    _delta_precompute(range(N_CHUNKS), qkv_ref, gate_ref, gatet_ref,
                      mq_ref, n_ref, oi_ref, dec_ref, consts)
    rec = [recurrence_chunk(c) for c in range(N_CHUNKS)]
    half = N_CHUNKS // 2
    for task in rec[:half] + [output_rows(0, TS // 2)] + rec[half:] + [output_rows(TS // 2, TS)]:
        task()


def _lane_row(vals, lane0):
    return jnp.zeros((1, TAIL), jnp.float32).at[0, lane0:lane0 + vals.shape[0]].set(vals)


def kernel(x, norm_w, w_in, pool_w, pool_scale, conv_w, a_log, dt_bias, dn_norm_w, w_out, final_norm_w):
    B, S, D = x.shape
    assert D == D_MODEL and S % TS == 0 and norm_w.shape[0] == 1 and w_in.shape == (1, D_MODEL, D_IN)

    def const(shape):
        return pl.BlockSpec(shape, lambda b, s: (0,) * len(shape), pipeline_mode=pl.Buffered(1))

    blk = pl.BlockSpec((1, TS, D_MODEL), lambda b, s: (b, s, 0))
    n_groups = len(POOL_WINDOWS)
    return pl.pallas_call(
        _block_kernel,
        grid=(B, S // TS),
        in_specs=[
            blk,
            const((1, D_MODEL)),
            const((1, D_MODEL, D_IN)),
            const((1, n_groups, POOL_GROUP, POOL_GROUP)),
            const((1, D_POOL)),
            const((CONV_WIDTH, 3 * D_DN)),
            const((1, TAIL)),
            const((1, TAIL)),
            const((1, HEAD_DIM)),
            const((1, D_MODEL, D_MODEL)),
            const((1, D_MODEL)),
        ],
        out_specs=blk,
        out_shape=jax.ShapeDtypeStruct(x.shape, x.dtype),
        scratch_shapes=[
            pltpu.VMEM((D_MODEL, D_MAIN), jnp.bfloat16),
            pltpu.VMEM((D_MODEL, TAIL), jnp.bfloat16),
            pltpu.VMEM((D_MODEL, D_MODEL), jnp.bfloat16),
            pltpu.VMEM((n_groups, POOL_GROUP, POOL_GROUP), jnp.bfloat16),
        ] + [pltpu.VMEM((GROUP_SLABS, HALO + TS, 128), jnp.float32)
             for _ in range(N_SLABS // GROUP_SLABS)] + [
            pltpu.VMEM((2, HALO + TS, 128), jnp.float32),
            pltpu.VMEM((3 * N_HEADS, TS, HEAD_DIM), jnp.float32),
            pltpu.VMEM((TS, TAIL), jnp.float32),
            pltpu.VMEM((TAIL, TS), jnp.float32),
            pltpu.VMEM((TS, D_MODEL), jnp.bfloat16),
            pltpu.VMEM((N_HEADS, HEAD_DIM, HEAD_DIM), jnp.float32),
            pltpu.VMEM((N_CHUNKS, N_HEADS, HEAD_DIM + CHUNK, HEAD_DIM), jnp.bfloat16),
            pltpu.VMEM((N_CHUNKS, N_HEADS, HEAD_DIM, HEAD_DIM), jnp.float32),
            pltpu.VMEM((N_CHUNKS, N_HEADS, CHUNK, HEAD_DIM), jnp.float32),
            pltpu.VMEM((N_CHUNKS, N_HEADS, 8, HEAD_DIM), jnp.float32),
        ],
        compiler_params=pltpu.CompilerParams(
            dimension_semantics=("arbitrary", "arbitrary"),
            vmem_limit_bytes=VMEM_LIMIT_BYTES),
        name="hymba_block",
    )(x, norm_w, w_in, pool_w, pool_scale, conv_w[0],
      _lane_row(a_log[0], LANE_DECAY), _lane_row(dt_bias[0], LANE_DECAY),
      dn_norm_w, w_out, final_norm_w[None, :])
```

```python
import jax
import jax.numpy as jnp
from jax import lax
from jax.experimental import pallas as pl
from jax.experimental.pallas import tpu as pltpu

D_MODEL = 1024
D_POOL = 512
D_DN = 512
POOL_WINDOWS = (2, 4, 8, 16)
POOL_GROUP = 128
HEAD_DIM = 128
N_HEADS = 4
CONV_WIDTH = 4
NORM_EPS = 1e-6

D_MAIN = 2 * D_POOL + 4 * D_DN
D_IN = D_MAIN + 2 * N_HEADS
TAIL = 128
OFF_PU, OFF_PZ, OFF_Q, OFF_K, OFF_V, OFF_DZ = 0, 512, 1024, 1536, 2048, 2560
LANE_BETA, LANE_DECAY = 0, N_HEADS

HALO = 24
POOL_EXT = 16
N_SLABS = (2 * D_POOL + 4 * D_DN) // 128
GROUP_SLABS = 4
CHUNK = 64
TS = 512
N_CHUNKS = TS // CHUNK
PROJ_COLS = 512
W_STAGE_ROWS = 128

VMEM_LIMIT_BYTES = 56 * 1024 * 1024


def _bdot(a, b):
    return jnp.dot(a.astype(jnp.bfloat16), b.astype(jnp.bfloat16),
                   preferred_element_type=jnp.float32)


def _bdot_nt(a, b):
    return lax.dot_general(a.astype(jnp.bfloat16), b.astype(jnp.bfloat16),
                           (((1,), (1,)), ((), ())), preferred_element_type=jnp.float32)


def _bdot_tn(a, b):
    return lax.dot_general(a.astype(jnp.bfloat16), b.astype(jnp.bfloat16),
                           (((0,), (0,)), ((), ())), preferred_element_type=jnp.float32)


def _silu(x):
    h = 0.5 * x
    return h + h * jnp.tanh(h)


def _packed_constants():
    shape = (CHUNK, N_HEADS * CHUNK)
    row = lax.broadcasted_iota(jnp.int32, shape, 0)
    lane = lax.broadcasted_iota(jnp.int32, shape, 1)
    col = lane % CHUNK
    blk = lane // CHUNK
    f32 = jnp.float32
    levels = []
    r = 1
    while r < CHUNK:
        levels.append((((row // (2 * r)) == (col // (2 * r)))
                       & ((row % (2 * r)) >= r) & ((col % (2 * r)) < r)).astype(f32))
        r *= 2
    return dict(
        causal=row >= col,
        strict=(row > col).astype(f32),
        eye=(row == col).astype(f32),
        levels=levels,
        blk=blk,
        head=[(blk == h).astype(jnp.bfloat16) for h in range(N_HEADS)],
    )


def _block_diag(p, consts):
    pb = p.astype(jnp.bfloat16)
    return jnp.concatenate([pb * m for m in consts["head"]], axis=0)


def _delta_precompute(chunks, qkv_ref, gate_ref, gatet_ref, mq_ref, n_ref, oi_ref, dec_ref, consts):
    f32, bf16 = jnp.float32, jnp.bfloat16
    heads = range(N_HEADS)

    def rows(c):
        return slice(c * CHUNK, (c + 1) * CHUNK)

    def lane_row(c, first):
        return jnp.concatenate([gatet_ref[first + h:first + h + 1, rows(c)] for h in heads], axis=1)

    def stack(parts):
        return jnp.concatenate(parts, axis=0)

    a_mats, t_mats, qkbs = [], [], []
    for c in chunks:
        brow = lane_row(c, LANE_BETA)
        grow = lane_row(c, LANE_DECAY)
        gcol = None
        for h in reversed(heads):
            bc = jnp.broadcast_to(gate_ref[rows(c), LANE_DECAY + h:LANE_DECAY + h + 1],
                                  (CHUNK, N_HEADS * CHUNK))
            gcol = bc if gcol is None else jnp.where(consts["blk"] == h, bc, gcol)
        db = jnp.exp(jnp.where(consts["causal"], gcol - grow, -jnp.inf)) * brow
        k16 = [qkv_ref[N_HEADS + h, rows(c), :].astype(bf16) for h in heads]
        q16 = [qkv_ref[h, rows(c), :].astype(bf16) for h in heads]
        zero = jnp.zeros((CHUNK, HEAD_DIM), bf16)
        kdiag = jnp.concatenate(
            [jnp.concatenate([k16[h] if j == h else zero for j in heads], axis=1) for h in heads], axis=0)
        lhs = jnp.concatenate([jnp.concatenate(k16, axis=1), jnp.concatenate(q16, axis=1)], axis=0)
        kq = lax.dot_general(lhs, kdiag, (((1,), (1,)), ((), ())), preferred_element_type=f32)
        a_mat = kq[:CHUNK] * db * consts["strict"]
        a_mats.append(a_mat)
        qkbs.append(kq[CHUNK:] * db)
        t_mats.append(consts["eye"] - a_mat * consts["levels"][0])
    g_mats = [a - jnp.dot(a.astype(bf16), _block_diag(a * consts["levels"][0], consts),
                          preferred_element_type=f32) for a in a_mats]
    for m in consts["levels"][1:]:
        tgs = [jnp.dot(jnp.concatenate([t, g], axis=0).astype(bf16), _block_diag(g * m, consts),
                       preferred_element_type=f32) for t, g in zip(t_mats, g_mats)]
        t_mats = [t - tg[:CHUNK] for t, tg in zip(t_mats, tgs)]
        g_mats = [g - tg[CHUNK:] for g, tg in zip(g_mats, tgs)]
    uws = []
    for c, t in zip(chunks, t_mats):
        v_stack = stack([qkv_ref[2 * N_HEADS + h, rows(c), :] for h in heads])
        kg_stack = stack([qkv_ref[N_HEADS + h, rows(c), :]
                          * jnp.exp(gate_ref[rows(c), LANE_DECAY + h:LANE_DECAY + h + 1]) for h in heads])
        rhs = jnp.concatenate([v_stack, kg_stack], axis=1).astype(bf16)
        uws.append(jnp.dot(_block_diag(t, consts), rhs, preferred_element_type=f32).astype(bf16))
    for c, uw, qkb in zip(chunks, uws, qkbs):
        qw = jnp.dot(_block_diag(qkb, consts), uw, preferred_element_type=f32)
        for h in heads:
            hs = slice(h * CHUNK, (h + 1) * CHUNK)
            bcol = gate_ref[rows(c), LANE_BETA + h:LANE_BETA + h + 1]
            gcol = gate_ref[rows(c), LANE_DECAY + h:LANE_DECAY + h + 1]
            glast = gate_ref[(c + 1) * CHUNK - 1:(c + 1) * CHUNK, LANE_DECAY + h:LANE_DECAY + h + 1]
            kdb = qkv_ref[N_HEADS + h, rows(c), :] * (jnp.exp(glast - gcol) * bcol)
            nm = _bdot_tn(kdb, uw[hs])
            n_ref[c, h] = nm[:, :HEAD_DIM]
            oi_ref[c, h] = qw[hs, :HEAD_DIM]
            q_eff = qkv_ref[h, rows(c), :] * jnp.exp(gcol) - qw[hs, HEAD_DIM:]
            mq_ref[c, h] = jnp.concatenate([nm[:, HEAD_DIM:], q_eff], axis=0).astype(bf16)
            dec_ref[c, h] = jnp.broadcast_to(jnp.exp(glast), (8, HEAD_DIM))


def _block_kernel(x_ref, norm_w_ref, w_in_ref, pool_w_ref, pool_scale_ref,
                  conv_w_ref, a_log_ref, dt_bias_ref, dn_norm_w_ref, w_out_ref, fnorm_w_ref,
                  out_ref,
                  w_all_ref, wo_ref, pw_ref,
                  pu_ref, pz_ref, pq_ref, pk_ref, pv_ref, pdz_ref,
                  ps_ref, qkv_ref, gate_ref, gatet_ref, y_ref, s_ref,
                  mq_ref, n_ref, oi_ref, dec_ref):
    sblk = pl.program_id(1)
    first_step = (pl.program_id(0) == 0) & (sblk == 0)
    group_refs = (pu_ref, pz_ref, pq_ref, pk_ref, pv_ref, pdz_ref)

    def pslab(slab):
        return group_refs[slab // GROUP_SLABS], slab % GROUP_SLABS

    @pl.when(first_step)
    def _():
        w_all_ref[:, D_MAIN:] = jnp.zeros((D_MODEL, TAIL), jnp.bfloat16)
        for r0 in range(0, D_MODEL, W_STAGE_ROWS):
            rs = slice(r0, r0 + W_STAGE_ROWS)
            w_all_ref[rs, 0:D_IN] = w_in_ref[0, rs, :].astype(jnp.bfloat16)
            wo_ref[rs, :] = w_out_ref[0, rs, :].astype(jnp.bfloat16)
        pw_ref[...] = jnp.zeros_like(pw_ref)
        for gi in range(len(POOL_WINDOWS)):
            d0 = (gi % 2) * POOL_GROUP
            pw_ref[gi // 2, d0:d0 + POOL_GROUP, d0:d0 + POOL_GROUP] = pool_w_ref[0, gi].astype(jnp.bfloat16)
        ps_ref[:, :, 0:HALO - POOL_EXT, :] = jnp.zeros((2, 2, HALO - POOL_EXT, 128), jnp.float32)

    @pl.when(sblk == 0)
    def _():
        for ref in group_refs:
            ref[:, 0:HALO, :] = jnp.zeros((GROUP_SLABS, HALO, 128), jnp.float32)
        s_ref[...] = jnp.zeros_like(s_ref)

    @pl.when(sblk != 0)
    def _():
        for ref in group_refs:
            ref[:, 0:HALO, :] = ref[:, TS:TS + HALO, :]

    x = x_ref[0]
    n = x * lax.rsqrt(jnp.mean(x * x, axis=-1, keepdims=True) + NORM_EPS) * norm_w_ref[...]
    nb = n.astype(jnp.bfloat16)

    res = jnp.dot(nb, w_all_ref[:, D_MAIN - 256:], preferred_element_type=jnp.float32)
    pdz_ref[GROUP_SLABS - 2, HALO:HALO + TS, :] = res[:, :128]
    pdz_ref[GROUP_SLABS - 1, HALO:HALO + TS, :] = res[:, 128:256]
    tail = res[:, 256:]

    gdec = -jnp.exp(a_log_ref[...]) * jax.nn.softplus(tail + dt_bias_ref[...])
    r_in_chunk = lax.broadcasted_iota(jnp.int32, (TS, TAIL), 0) % CHUNK
    gc = gdec
    k = 1
    while k < CHUNK:
        gc = gc + jnp.where(r_in_chunk >= k, pltpu.roll(gc, k, axis=0), 0.0)
        k *= 2
    lane_t = lax.broadcasted_iota(jnp.int32, (TS, TAIL), 1)
    gates = jnp.where(lane_t < LANE_DECAY, jax.nn.sigmoid(tail), gc)
    gate_ref[...] = gates
    gatet_ref[...] = gates.T

    def proj_pair(slab):
        def run():
            c0 = slab * 128
            res = jnp.dot(nb, w_all_ref[:, c0:c0 + 256], preferred_element_type=jnp.float32)
            ref, j = pslab(slab)
            ref[j, HALO:HALO + TS, :] = res[:, :128]
            ref[j + 1, HALO:HALO + TS, :] = res[:, 128:]
        return run

    def taps(ref, slab, first_row, n_rows):
        lead = slab if isinstance(slab, tuple) else (slab,)
        return ref[lead + (pl.ds(first_row, n_rows, stride=2), slice(None))]

    t_pos = sblk * TS + lax.broadcasted_iota(jnp.int32, (TS, 1), 0)

    def window_sum(gi, w):
        n_half = (POOL_EXT + TS) // 2
        src, src_slab = pu_ref, (gi,)
        k = 1
        level = 0
        while k < w:
            for par in range(2):
                first = HALO - POOL_EXT + par
                ssum = taps(src, src_slab, first, n_half) + taps(src, src_slab, first - k, n_half)
                ps_ref[gi % 2, level % 2, pl.ds(first, n_half, stride=2), :] = ssum
            src, src_slab = ps_ref, (gi % 2, level % 2)
            k *= 2
            level += 1
        return src[src_slab + (slice(HALO, HALO + TS), slice(None))]

    def pool_pair(pair):
        def run():
            mixes = []
            for gi in (2 * pair, 2 * pair + 1):
                w = POOL_WINDOWS[gi]
                inv_cnt = 1.0 / jnp.minimum(t_pos + 1, w).astype(jnp.float32)
                mixes.append(window_sum(gi, w) * inv_cnt - pu_ref[gi, HALO:HALO + TS, :])
            mixed = _bdot(jnp.concatenate(mixes, axis=1), pw_ref[pair])
            z = jnp.concatenate([pz_ref[gi, HALO:HALO + TS, :] for gi in (2 * pair, 2 * pair + 1)], axis=1)
            c0 = OFF_PU + 2 * pair * POOL_GROUP
            y_ref[:, c0:c0 + 2 * POOL_GROUP] = (
                mixed * pool_scale_ref[:, c0:c0 + 2 * POOL_GROUP] * _silu(z)).astype(jnp.bfloat16)
        return run

    def conv_group(ci):
        def run():
            slab = OFF_Q // 128 + ci
            cw = conv_w_ref[:, ci * HEAD_DIM:(ci + 1) * HEAD_DIM]
            for par in range(2):
                acc = None
                for j in range(CONV_WIDTH):
                    term = taps(*pslab(slab), HALO + par - (CONV_WIDTH - 1) + j, TS // 2) * cw[j:j + 1, :]
                    acc = term if acc is None else acc + term
                a = _silu(acc)
                if ci < 2 * N_HEADS:
                    a = a * lax.rsqrt(jnp.sum(a * a, axis=-1, keepdims=True) + NORM_EPS)
                if ci < N_HEADS:
                    a = a * (HEAD_DIM ** -0.5)
                qkv_ref[ci, pl.ds(par, TS // 2, stride=2), :] = a
        return run

    convs = [conv_group(ci) for ci in range(3 * N_HEADS)]
    pools = [None, pool_pair(0), None, pool_pair(1)]
    first_slab = {name: off // 128 for name, off in
                  dict(q=OFF_Q, k=OFF_K, v=OFF_V, pu=OFF_PU, pz=OFF_PZ, dz=OFF_DZ).items()}
    pair_order = [first_slab[name] + j for name in ("q", "k", "v", "pu", "pz", "dz") for j in (0, 2)][:-1]
    mixers = [None] * N_HEADS + convs + [None] * N_HEADS + pools
    for i, slab in enumerate(pair_order):
        proj_pair(slab)()
        for task in mixers[2 * i:2 * i + 2]:
            if task is not None:
                task()
    for task in mixers[2 * len(pair_order):]:
        if task is not None:
            task()
    consts = _packed_constants()

    def recurrence_chunk(c):
        def run():
            r0 = c * CHUNK
            for h in range(N_HEADS):
                state = s_ref[h]
                r = jnp.dot(mq_ref[c, h], state.astype(jnp.bfloat16), preferred_element_type=jnp.float32)
                s_ref[h] = state * dec_ref[c, h][0:1, :] + n_ref[c, h] - r[:HEAD_DIM]
                o = r[HEAD_DIM:] + oi_ref[c, h]
                o = o * lax.rsqrt(jnp.mean(o * o, axis=-1, keepdims=True) + NORM_EPS) * dn_norm_w_ref[...]
                dz = pdz_ref[h, HALO + r0:HALO + r0 + CHUNK, :]
                y_ref[r0:r0 + CHUNK, D_POOL + h * HEAD_DIM:D_POOL + (h + 1) * HEAD_DIM] = (
                    o * _silu(dz)).astype(jnp.bfloat16)
        return run

    def output_rows(r0, r1):
        def run():
            hres = x_ref[0, r0:r1, :] + jnp.dot(y_ref[r0:r1, :], wo_ref[...],
                                                 preferred_element_type=jnp.float32)
            out_ref[0, r0:r1, :] = (
                hres * lax.rsqrt(jnp.mean(hres * hres, axis=-1, keepdims=True) + NORM_EPS)
                * fnorm_w_ref[...])
        return run

    _delta_precompute(range(N_CHUNKS), qkv_ref, gate_ref, gatet_ref,
                      mq_ref, n_ref, oi_ref, dec_ref, consts)
    rec = [recurrence_chunk(c) for c in range(N_CHUNKS)]
    half = N_CHUNKS // 2
    for task in rec + [output_rows(0, TS)]:
        task()


def _lane_row(vals, lane0):
    return jnp.zeros((1, TAIL), jnp.float32).at[0, lane0:lane0 + vals.shape[0]].set(vals)


def kernel(x, norm_w, w_in, pool_w, pool_scale, conv_w, a_log, dt_bias, dn_norm_w, w_out, final_norm_w):
    B, S, D = x.shape
    assert D == D_MODEL and S % TS == 0 and norm_w.shape[0] == 1 and w_in.shape == (1, D_MODEL, D_IN)

    def const(shape):
        return pl.BlockSpec(shape, lambda b, s: (0,) * len(shape), pipeline_mode=pl.Buffered(1))

    blk = pl.BlockSpec((1, TS, D_MODEL), lambda b, s: (b, s, 0))
    n_groups = len(POOL_WINDOWS)
    return pl.pallas_call(
        _block_kernel,
        grid=(B, S // TS),
        in_specs=[
            blk,
            const((1, D_MODEL)),
            const((1, D_MODEL, D_IN)),
            const((1, n_groups, POOL_GROUP, POOL_GROUP)),
            const((1, D_POOL)),
            const((CONV_WIDTH, 3 * D_DN)),
            const((1, TAIL)),
            const((1, TAIL)),
            const((1, HEAD_DIM)),
            const((1, D_MODEL, D_MODEL)),
            const((1, D_MODEL)),
        ],
        out_specs=blk,
        out_shape=jax.ShapeDtypeStruct(x.shape, x.dtype),
        scratch_shapes=[
            pltpu.VMEM((D_MODEL, D_MAIN + TAIL), jnp.bfloat16),
            pltpu.VMEM((D_MODEL, D_MODEL), jnp.bfloat16),
            pltpu.VMEM((n_groups // 2, 2 * POOL_GROUP, 2 * POOL_GROUP), jnp.bfloat16),
        ] + [pltpu.VMEM((GROUP_SLABS, HALO + TS, 128), jnp.float32)
             for _ in range(N_SLABS // GROUP_SLABS)] + [
            pltpu.VMEM((2, 2, HALO + TS, 128), jnp.float32),
            pltpu.VMEM((3 * N_HEADS, TS, HEAD_DIM), jnp.float32),
            pltpu.VMEM((TS, TAIL), jnp.float32),
            pltpu.VMEM((TAIL, TS), jnp.float32),
            pltpu.VMEM((TS, D_MODEL), jnp.bfloat16),
            pltpu.VMEM((N_HEADS, HEAD_DIM, HEAD_DIM), jnp.float32),
            pltpu.VMEM((N_CHUNKS, N_HEADS, HEAD_DIM + CHUNK, HEAD_DIM), jnp.bfloat16),
            pltpu.VMEM((N_CHUNKS, N_HEADS, HEAD_DIM, HEAD_DIM), jnp.float32),
            pltpu.VMEM((N_CHUNKS, N_HEADS, CHUNK, HEAD_DIM), jnp.float32),
            pltpu.VMEM((N_CHUNKS, N_HEADS, 8, HEAD_DIM), jnp.float32),
        ],
        compiler_params=pltpu.CompilerParams(
            dimension_semantics=("arbitrary", "arbitrary"),
            vmem_limit_bytes=VMEM_LIMIT_BYTES),
        name="hymba_block",
    )(x, norm_w, w_in, pool_w, pool_scale, conv_w[0],
      _lane_row(a_log[0], LANE_DECAY), _lane_row(dt_bias[0], LANE_DECAY),
      dn_norm_w, w_out, final_norm_w[None, :])
```

```python
import jax
import jax.numpy as jnp
from jax import lax
from jax.experimental import pallas as pl
from jax.experimental.pallas import tpu as pltpu

D_MODEL = 1024
D_POOL = 512
D_DN = 512
POOL_WINDOWS = (2, 4, 8, 16)
POOL_GROUP = 128
HEAD_DIM = 128
N_HEADS = 4
CONV_WIDTH = 4
NORM_EPS = 1e-6

D_MAIN = 2 * D_POOL + 4 * D_DN
D_IN = D_MAIN + 2 * N_HEADS
TAIL = 128
OFF_PU, OFF_PZ, OFF_Q, OFF_K, OFF_V, OFF_DZ = 0, 512, 1024, 1536, 2048, 2560
LANE_BETA, LANE_DECAY = 0, N_HEADS

HALO = 24
POOL_EXT = 16
N_SLABS = (2 * D_POOL + 4 * D_DN) // 128
GROUP_SLABS = 4
CHUNK = 64
TS = 512
ROWS = 4
TB = TS // ROWS
SEG = HALO + TB
EXT = ROWS * SEG
N_CHUNKS = TS // CHUNK
ROW_CHUNKS = TB // CHUNK
PROJ_COLS = 512
W_STAGE_ROWS = 128

VMEM_LIMIT_BYTES = 56 * 1024 * 1024


def _bdot(a, b):
    return jnp.dot(a.astype(jnp.bfloat16), b.astype(jnp.bfloat16),
                   preferred_element_type=jnp.float32)


def _bdot_nt(a, b):
    return lax.dot_general(a.astype(jnp.bfloat16), b.astype(jnp.bfloat16),
                           (((1,), (1,)), ((), ())), preferred_element_type=jnp.float32)


def _bdot_tn(a, b):
    return lax.dot_general(a.astype(jnp.bfloat16), b.astype(jnp.bfloat16),
                           (((0,), (0,)), ((), ())), preferred_element_type=jnp.float32)


def _slab_row(r):
    return (r // TB) * SEG + HALO + r % TB


def _from_slab(ref, lead):
    return jnp.concatenate([ref[lead + (slice(rb * SEG + HALO, (rb + 1) * SEG), slice(None))]
                            for rb in range(ROWS)], axis=0)


def _to_slab(ref, lead, val):
    for rb in range(ROWS):
        ref[lead + (slice(rb * SEG + HALO, (rb + 1) * SEG), slice(None))] = val[rb * TB:(rb + 1) * TB]


def _silu(x):
    h = 0.5 * x
    return h + h * jnp.tanh(h)


def _packed_constants():
    shape = (CHUNK, N_HEADS * CHUNK)
    row = lax.broadcasted_iota(jnp.int32, shape, 0)
    lane = lax.broadcasted_iota(jnp.int32, shape, 1)
    col = lane % CHUNK
    blk = lane // CHUNK
    f32 = jnp.float32
    levels = []
    r = 1
    while r < CHUNK:
        levels.append((((row // (2 * r)) == (col // (2 * r)))
                       & ((row % (2 * r)) >= r) & ((col % (2 * r)) < r)).astype(f32))
        r *= 2
    return dict(
        causal=row >= col,
        strict=(row > col).astype(f32),
        eye=(row == col).astype(f32),
        levels=levels,
        blk=blk,
        head=[(blk == h).astype(jnp.bfloat16) for h in range(N_HEADS)],
    )


def _block_diag(p, consts):
    pb = p.astype(jnp.bfloat16)
    return jnp.concatenate([pb * m for m in consts["head"]], axis=0)


def _delta_precompute(chunks, qkv_ref, gate_ref, gatet_ref, mq_ref, n_ref, oi_ref, dec_ref, consts):
    f32, bf16 = jnp.float32, jnp.bfloat16
    heads = range(N_HEADS)

    def rows(c):
        return slice(c * CHUNK, (c + 1) * CHUNK)

    def srows(c):
        start = _slab_row(c * CHUNK)
        return slice(start, start + CHUNK)

    def lane_row(c, first):
        return jnp.concatenate([gatet_ref[first + h:first + h + 1, rows(c)] for h in heads], axis=1)

    def stack(parts):
        return jnp.concatenate(parts, axis=0)

    a_mats, t_mats, qkbs = [], [], []
    for c in chunks:
        brow = lane_row(c, LANE_BETA)
        grow = lane_row(c, LANE_DECAY)
        gcol = None
        for h in reversed(heads):
            bc = jnp.broadcast_to(gate_ref[rows(c), LANE_DECAY + h:LANE_DECAY + h + 1],
                                  (CHUNK, N_HEADS * CHUNK))
            gcol = bc if gcol is None else jnp.where(consts["blk"] == h, bc, gcol)
        db = jnp.exp(jnp.where(consts["causal"], gcol - grow, -jnp.inf)) * brow
        k16 = [qkv_ref[N_HEADS + h, srows(c), :].astype(bf16) for h in heads]
        q16 = [qkv_ref[h, srows(c), :].astype(bf16) for h in heads]
        zero = jnp.zeros((CHUNK, HEAD_DIM), bf16)
        kdiag = jnp.concatenate(
            [jnp.concatenate([k16[h] if j == h else zero for j in heads], axis=1) for h in heads], axis=0)
        lhs = jnp.concatenate([jnp.concatenate(k16, axis=1), jnp.concatenate(q16, axis=1)], axis=0)
        kq = lax.dot_general(lhs, kdiag, (((1,), (1,)), ((), ())), preferred_element_type=f32)
        a_mat = kq[:CHUNK] * db * consts["strict"]
        a_mats.append(a_mat)
        qkbs.append(kq[CHUNK:] * db)
        t_mats.append(consts["eye"] - a_mat * consts["levels"][0])
    g_mats = [a - jnp.dot(a.astype(bf16), _block_diag(a * consts["levels"][0], consts),
                          preferred_element_type=f32) for a in a_mats]
    for m in consts["levels"][1:]:
        tgs = [jnp.dot(jnp.concatenate([t, g], axis=0).astype(bf16), _block_diag(g * m, consts),
                       preferred_element_type=f32) for t, g in zip(t_mats, g_mats)]
        t_mats = [t - tg[:CHUNK] for t, tg in zip(t_mats, tgs)]
        g_mats = [g - tg[CHUNK:] for g, tg in zip(g_mats, tgs)]
    uws = []
    for c, t in zip(chunks, t_mats):
        v_stack = stack([qkv_ref[2 * N_HEADS + h, srows(c), :] for h in heads])
        kg_stack = stack([qkv_ref[N_HEADS + h, srows(c), :]
                          * jnp.exp(gate_ref[rows(c), LANE_DECAY + h:LANE_DECAY + h + 1]) for h in heads])
        rhs = jnp.concatenate([v_stack, kg_stack], axis=1).astype(bf16)
        uws.append(jnp.dot(_block_diag(t, consts), rhs, preferred_element_type=f32).astype(bf16))
    for c, uw, qkb in zip(chunks, uws, qkbs):
        qw = jnp.dot(_block_diag(qkb, consts), uw, preferred_element_type=f32)
        for h in heads:
            hs = slice(h * CHUNK, (h + 1) * CHUNK)
            bcol = gate_ref[rows(c), LANE_BETA + h:LANE_BETA + h + 1]
            gcol = gate_ref[rows(c), LANE_DECAY + h:LANE_DECAY + h + 1]
            glast = gate_ref[(c + 1) * CHUNK - 1:(c + 1) * CHUNK, LANE_DECAY + h:LANE_DECAY + h + 1]
            kdb = qkv_ref[N_HEADS + h, srows(c), :] * (jnp.exp(glast - gcol) * bcol)
            nm = _bdot_tn(kdb, uw[hs])
            n_ref[c, h] = nm[:, :HEAD_DIM]
            oi_ref[c, h] = qw[hs, :HEAD_DIM]
            q_eff = qkv_ref[h, srows(c), :] * jnp.exp(gcol) - qw[hs, HEAD_DIM:]
            mq_ref[c, h] = jnp.concatenate([nm[:, HEAD_DIM:], q_eff], axis=0).astype(bf16)
            dec_ref[c, h] = jnp.broadcast_to(jnp.exp(glast), (8, HEAD_DIM))


def _block_kernel(x_ref, norm_w_ref, w_in_ref, pool_w_ref, pool_scale_ref,
                  conv_w_ref, a_log_ref, dt_bias_ref, dn_norm_w_ref, w_out_ref, fnorm_w_ref,
                  out_ref,
                  w_all_ref, wo_ref, pw_ref,
                  pu_ref, pz_ref, pq_ref, pk_ref, pv_ref, pdz_ref,
                  ps_ref, qkv_ref, gate_ref, gatet_ref, y_ref, s_ref,
                  mq_ref, n_ref, oi_ref, dec_ref):
    sblk = pl.program_id(1)
    first_step = (pl.program_id(0) == 0) & (sblk == 0)
    group_refs = (pu_ref, pz_ref, pq_ref, pk_ref, pv_ref, pdz_ref)

    def pslab(slab):
        return group_refs[slab // GROUP_SLABS], slab % GROUP_SLABS

    @pl.when(first_step)
    def _():
        w_all_ref[:, D_MAIN:] = jnp.zeros((D_MODEL, TAIL), jnp.bfloat16)
        for r0 in range(0, D_MODEL, W_STAGE_ROWS):
            rs = slice(r0, r0 + W_STAGE_ROWS)
            w_all_ref[rs, 0:D_IN] = w_in_ref[0, rs, :].astype(jnp.bfloat16)
            wo_ref[rs, :] = w_out_ref[0, rs, :].astype(jnp.bfloat16)
        pw_ref[...] = jnp.zeros_like(pw_ref)
        for gi in range(len(POOL_WINDOWS)):
            d0 = (gi % 2) * POOL_GROUP
            pw_ref[gi // 2, d0:d0 + POOL_GROUP, d0:d0 + POOL_GROUP] = pool_w_ref[0, gi].astype(jnp.bfloat16)
        ps_ref[:, :, 0:HALO - POOL_EXT, :] = jnp.zeros((2, 2, HALO - POOL_EXT, 128), jnp.float32)

    @pl.when(sblk == 0)
    def _():
        for ref in group_refs:
            for rb in range(ROWS):
                ref[:, rb * SEG:rb * SEG + HALO, :] = jnp.zeros((GROUP_SLABS, HALO, 128), jnp.float32)
        s_ref[...] = jnp.zeros_like(s_ref)

    @pl.when(sblk != 0)
    def _():
        for ref in group_refs:
            for rb in range(ROWS):
                ref[:, rb * SEG:rb * SEG + HALO, :] = ref[:, rb * SEG + TB:(rb + 1) * SEG, :]

    x = x_ref[...].reshape(TS, D_MODEL)
    n = x * lax.rsqrt(jnp.mean(x * x, axis=-1, keepdims=True) + NORM_EPS) * norm_w_ref[...]
    nb = n.astype(jnp.bfloat16)

    res = jnp.dot(nb, w_all_ref[:, D_MAIN - 256:], preferred_element_type=jnp.float32)
    _to_slab(pdz_ref, (GROUP_SLABS - 2,), res[:, :128])
    _to_slab(pdz_ref, (GROUP_SLABS - 1,), res[:, 128:256])
    tail = res[:, 256:]

    gdec = -jnp.exp(a_log_ref[...]) * jax.nn.softplus(tail + dt_bias_ref[...])
    r_in_chunk = lax.broadcasted_iota(jnp.int32, (TS, TAIL), 0) % CHUNK
    gc = gdec
    k = 1
    while k < CHUNK:
        gc = gc + jnp.where(r_in_chunk >= k, pltpu.roll(gc, k, axis=0), 0.0)
        k *= 2
    lane_t = lax.broadcasted_iota(jnp.int32, (TS, TAIL), 1)
    gates = jnp.where(lane_t < LANE_DECAY, jax.nn.sigmoid(tail), gc)
    gate_ref[...] = gates
    gatet_ref[...] = gates.T

    def proj_pair(slab):
        def run():
            c0 = slab * 128
            res = jnp.dot(nb, w_all_ref[:, c0:c0 + 256], preferred_element_type=jnp.float32)
            ref, j = pslab(slab)
            _to_slab(ref, (j,), res[:, :128])
            _to_slab(ref, (j + 1,), res[:, 128:])
        return run

    def taps(ref, slab, first_row, n_rows):
        lead = slab if isinstance(slab, tuple) else (slab,)
        return ref[lead + (pl.ds(first_row, n_rows, stride=2), slice(None))]

    t_pos = sblk * TB + lax.broadcasted_iota(jnp.int32, (TS, 1), 0) % TB

    def window_sum(gi, w):
        n_half = (EXT - (HALO - POOL_EXT)) // 2
        src, src_slab = pu_ref, (gi,)
        k = 1
        level = 0
        while k < w:
            for par in range(2):
                first = HALO - POOL_EXT + par
                ssum = taps(src, src_slab, first, n_half) + taps(src, src_slab, first - k, n_half)
                ps_ref[gi % 2, level % 2, pl.ds(first, n_half, stride=2), :] = ssum
            src, src_slab = ps_ref, (gi % 2, level % 2)
            k *= 2
            level += 1
        return _from_slab(src, src_slab)

    def pool_pair(pair):
        def run():
            mixes = []
            for gi in (2 * pair, 2 * pair + 1):
                w = POOL_WINDOWS[gi]
                inv_cnt = 1.0 / jnp.minimum(t_pos + 1, w).astype(jnp.float32)
                mixes.append(window_sum(gi, w) * inv_cnt - _from_slab(pu_ref, (gi,)))
            mixed = _bdot(jnp.concatenate(mixes, axis=1), pw_ref[pair])
            z = jnp.concatenate([_from_slab(pz_ref, (gi,)) for gi in (2 * pair, 2 * pair + 1)], axis=1)
            c0 = OFF_PU + 2 * pair * POOL_GROUP
            y_ref[:, c0:c0 + 2 * POOL_GROUP] = (
                mixed * pool_scale_ref[:, c0:c0 + 2 * POOL_GROUP] * _silu(z)).astype(jnp.bfloat16)
        return run

    def conv_group(ci):
        def run():
            slab = OFF_Q // 128 + ci
            cw = conv_w_ref[:, ci * HEAD_DIM:(ci + 1) * HEAD_DIM]
            for par in range(2):
                acc = None
                for j in range(CONV_WIDTH):
                    term = (taps(*pslab(slab), HALO + par - (CONV_WIDTH - 1) + j, (EXT - HALO) // 2)
                            * cw[j:j + 1, :])
                    acc = term if acc is None else acc + term
                a = _silu(acc)
                if ci < 2 * N_HEADS:
                    a = a * lax.rsqrt(jnp.sum(a * a, axis=-1, keepdims=True) + NORM_EPS)
                if ci < N_HEADS:
                    a = a * (HEAD_DIM ** -0.5)
                qkv_ref[ci, pl.ds(HALO + par, (EXT - HALO) // 2, stride=2), :] = a
        return run

    convs = [conv_group(ci) for ci in range(3 * N_HEADS)]
    pools = [None, pool_pair(0), None, pool_pair(1)]
    first_slab = {name: off // 128 for name, off in
                  dict(q=OFF_Q, k=OFF_K, v=OFF_V, pu=OFF_PU, pz=OFF_PZ, dz=OFF_DZ).items()}
    pair_order = [first_slab[name] + j for name in ("q", "k", "v", "pu", "pz", "dz") for j in (0, 2)][:-1]
    mixers = [None] * N_HEADS + convs + [None] * N_HEADS + pools
    for i, slab in enumerate(pair_order):
        proj_pair(slab)()
        for task in mixers[2 * i:2 * i + 2]:
            if task is not None:
                task()
    for task in mixers[2 * len(pair_order):]:
        if task is not None:
            task()
    consts = _packed_constants()

    def recurrence_chunk(c):
        def run():
            r0 = c * CHUNK
            for h in range(N_HEADS):
                hs = (c // ROW_CHUNKS) * N_HEADS + h
                state = s_ref[hs]
                r = jnp.dot(mq_ref[c, h], state.astype(jnp.bfloat16), preferred_element_type=jnp.float32)
                s_ref[hs] = state * dec_ref[c, h][0:1, :] + n_ref[c, h] - r[:HEAD_DIM]
                o = r[HEAD_DIM:] + oi_ref[c, h]
                o = o * lax.rsqrt(jnp.mean(o * o, axis=-1, keepdims=True) + NORM_EPS) * dn_norm_w_ref[...]
                dz = pdz_ref[h, _slab_row(r0):_slab_row(r0) + CHUNK, :]
                y_ref[r0:r0 + CHUNK, D_POOL + h * HEAD_DIM:D_POOL + (h + 1) * HEAD_DIM] = (
                    o * _silu(dz)).astype(jnp.bfloat16)
        return run

    def output_rows(r0, r1):
        def run():
            hres = x_ref[...].reshape(TS, D_MODEL)[r0:r1, :] + jnp.dot(y_ref[r0:r1, :], wo_ref[...],
                                                 preferred_element_type=jnp.float32)
            out_ref[...] = (hres * lax.rsqrt(jnp.mean(hres * hres, axis=-1, keepdims=True) + NORM_EPS)
                            * fnorm_w_ref[...]).reshape(ROWS, TB, D_MODEL)
        return run

    _delta_precompute(range(N_CHUNKS), qkv_ref, gate_ref, gatet_ref,
                      mq_ref, n_ref, oi_ref, dec_ref, consts)
    rec = [recurrence_chunk(rb * ROW_CHUNKS + cc) for cc in range(ROW_CHUNKS) for rb in range(ROWS)]
    half = N_CHUNKS // 2
    for task in rec + [output_rows(0, TS)]:
        task()


def _lane_row(vals, lane0):
    return jnp.zeros((1, TAIL), jnp.float32).at[0, lane0:lane0 + vals.shape[0]].set(vals)


def kernel(x, norm_w, w_in, pool_w, pool_scale, conv_w, a_log, dt_bias, dn_norm_w, w_out, final_norm_w):
    B, S, D = x.shape
    assert D == D_MODEL and S % TB == 0 and B % ROWS == 0 and norm_w.shape[0] == 1
    assert w_in.shape == (1, D_MODEL, D_IN)

    def const(shape):
        return pl.BlockSpec(shape, lambda b, s: (0,) * len(shape), pipeline_mode=pl.Buffered(1))

    blk = pl.BlockSpec((ROWS, TB, D_MODEL), lambda b, s: (b, s, 0))
    n_groups = len(POOL_WINDOWS)
    return pl.pallas_call(
        _block_kernel,
        grid=(B // ROWS, S // TB),
        in_specs=[
            blk,
            const((1, D_MODEL)),
            const((1, D_MODEL, D_IN)),
            const((1, n_groups, POOL_GROUP, POOL_GROUP)),
            const((1, D_POOL)),
            const((CONV_WIDTH, 3 * D_DN)),
            const((1, TAIL)),
            const((1, TAIL)),
            const((1, HEAD_DIM)),
            const((1, D_MODEL, D_MODEL)),
            const((1, D_MODEL)),
        ],
        out_specs=blk,
        out_shape=jax.ShapeDtypeStruct(x.shape, x.dtype),
        scratch_shapes=[
            pltpu.VMEM((D_MODEL, D_MAIN + TAIL), jnp.bfloat16),
            pltpu.VMEM((D_MODEL, D_MODEL), jnp.bfloat16),
            pltpu.VMEM((n_groups // 2, 2 * POOL_GROUP, 2 * POOL_GROUP), jnp.bfloat16),
        ] + [pltpu.VMEM((GROUP_SLABS, EXT, 128), jnp.float32)
             for _ in range(N_SLABS // GROUP_SLABS)] + [
            pltpu.VMEM((2, 2, EXT, 128), jnp.float32),
            pltpu.VMEM((3 * N_HEADS, EXT, HEAD_DIM), jnp.float32),
            pltpu.VMEM((TS, TAIL), jnp.float32),
            pltpu.VMEM((TAIL, TS), jnp.float32),
            pltpu.VMEM((TS, D_MODEL), jnp.bfloat16),
            pltpu.VMEM((ROWS * N_HEADS, HEAD_DIM, HEAD_DIM), jnp.float32),
            pltpu.VMEM((N_CHUNKS, N_HEADS, HEAD_DIM + CHUNK, HEAD_DIM), jnp.bfloat16),
            pltpu.VMEM((N_CHUNKS, N_HEADS, HEAD_DIM, HEAD_DIM), jnp.float32),
            pltpu.VMEM((N_CHUNKS, N_HEADS, CHUNK, HEAD_DIM), jnp.float32),
            pltpu.VMEM((N_CHUNKS, N_HEADS, 8, HEAD_DIM), jnp.float32),
        ],
        compiler_params=pltpu.CompilerParams(
            dimension_semantics=("arbitrary", "arbitrary"),
            vmem_limit_bytes=VMEM_LIMIT_BYTES),
        name="hymba_block",
    )(x, norm_w, w_in, pool_w, pool_scale, conv_w[0],
      _lane_row(a_log[0], LANE_DECAY), _lane_row(dt_bias[0], LANE_DECAY),
      dn_norm_w, w_out, final_norm_w[None, :])
```

```python
import jax
import jax.numpy as jnp
from jax import lax
from jax.experimental import pallas as pl
from jax.experimental.pallas import tpu as pltpu

D_MODEL = 1024
D_POOL = 512
D_DN = 512
POOL_WINDOWS = (2, 4, 8, 16)
POOL_GROUP = 128
HEAD_DIM = 128
N_HEADS = 4
CONV_WIDTH = 4
NORM_EPS = 1e-6

D_MAIN = 2 * D_POOL + 4 * D_DN
D_IN = D_MAIN + 2 * N_HEADS
TAIL = 128
OFF_PU, OFF_PZ, OFF_Q, OFF_K, OFF_V, OFF_DZ = 0, 512, 1024, 1536, 2048, 2560
LANE_BETA, LANE_DECAY = 0, N_HEADS

HALO = 24
POOL_EXT = 16
N_SLABS = (2 * D_POOL + 4 * D_DN) // 128
GROUP_SLABS = 4
CHUNK = 64
TS = 512
ROWS = 4
TB = TS // ROWS
SEG = HALO + TB
EXT = ROWS * SEG
N_CHUNKS = TS // CHUNK
ROW_CHUNKS = TB // CHUNK
PROJ_COLS = 256
W_STAGE_ROWS = 128

VMEM_LIMIT_BYTES = 56 * 1024 * 1024


def _bdot(a, b):
    return jnp.dot(a.astype(jnp.bfloat16), b.astype(jnp.bfloat16),
                   preferred_element_type=jnp.float32)


def _bdot_tn(a, b):
    return lax.dot_general(a.astype(jnp.bfloat16), b.astype(jnp.bfloat16),
                           (((0,), (0,)), ((), ())), preferred_element_type=jnp.float32)


def _slab_row(r):
    return (r // TB) * SEG + HALO + r % TB


def _from_slab(ref, lead):
    return jnp.concatenate([ref[lead + (slice(rb * SEG + HALO, (rb + 1) * SEG), slice(None))]
                            for rb in range(ROWS)], axis=0)


def _to_slab(ref, lead, val):
    for rb in range(ROWS):
        ref[lead + (slice(rb * SEG + HALO, (rb + 1) * SEG), slice(None))] = val[rb * TB:(rb + 1) * TB]


def _silu(x):
    h = 0.5 * x
    return h + h * jnp.tanh(h)


def _packed_constants():
    shape = (CHUNK, N_HEADS * CHUNK)
    row = lax.broadcasted_iota(jnp.int32, shape, 0)
    lane = lax.broadcasted_iota(jnp.int32, shape, 1)
    col = lane % CHUNK
    blk = lane // CHUNK
    f32 = jnp.float32
    levels = []
    r = 1
    while r < CHUNK:
        levels.append((((row // (2 * r)) == (col // (2 * r)))
                       & ((row % (2 * r)) >= r) & ((col % (2 * r)) < r)).astype(f32))
        r *= 2
    return dict(
        causal=row >= col,
        strict=(row > col).astype(f32),
        eye=(row == col).astype(f32),
        levels=levels,
        blk=blk,
        head=[(blk == h).astype(jnp.bfloat16) for h in range(N_HEADS)],
    )


def _block_diag(p, consts):
    pb = p.astype(jnp.bfloat16)
    return jnp.concatenate([pb * m for m in consts["head"]], axis=0)


def _delta_precompute(chunks, qkv_ref, gate_ref, gatet_ref, mq_ref, n_ref, oi_ref, dec_ref, consts):
    f32, bf16 = jnp.float32, jnp.bfloat16
    heads = range(N_HEADS)

    def rows(c):
        return slice(c * CHUNK, (c + 1) * CHUNK)

    def srows(c):
        start = _slab_row(c * CHUNK)
        return slice(start, start + CHUNK)

    def lane_row(c, first):
        return jnp.concatenate([gatet_ref[first + h:first + h + 1, rows(c)] for h in heads], axis=1)

    def stack(parts):
        return jnp.concatenate(parts, axis=0)

    a_mats, t_mats, qkbs = [], [], []
    for c in chunks:
        brow = lane_row(c, LANE_BETA)
        grow = lane_row(c, LANE_DECAY)
        gcol = None
        for h in reversed(heads):
            bc = jnp.broadcast_to(gate_ref[rows(c), LANE_DECAY + h:LANE_DECAY + h + 1],
                                  (CHUNK, N_HEADS * CHUNK))
            gcol = bc if gcol is None else jnp.where(consts["blk"] == h, bc, gcol)
        db = jnp.exp(jnp.where(consts["causal"], gcol - grow, -jnp.inf)) * brow
        k16 = [qkv_ref[N_HEADS + h, srows(c), :].astype(bf16) for h in heads]
        q16 = [qkv_ref[h, srows(c), :].astype(bf16) for h in heads]
        zero = jnp.zeros((CHUNK, HEAD_DIM), bf16)
        kdiag = jnp.concatenate(
            [jnp.concatenate([k16[h] if j == h else zero for j in heads], axis=1) for h in heads], axis=0)
        lhs = jnp.concatenate([jnp.concatenate(k16, axis=1), jnp.concatenate(q16, axis=1)], axis=0)
        kq = lax.dot_general(lhs, kdiag, (((1,), (1,)), ((), ())), preferred_element_type=f32)
        a_mat = kq[:CHUNK] * db * consts["strict"]
        a_mats.append(a_mat)
        qkbs.append(kq[CHUNK:] * db)
        t_mats.append(consts["eye"] - a_mat * consts["levels"][0])
    g_mats = [a - jnp.dot(a.astype(bf16), _block_diag(a * consts["levels"][0], consts),
                          preferred_element_type=f32) for a in a_mats]
    for m in consts["levels"][1:]:
        tgs = [jnp.dot(jnp.concatenate([t, g], axis=0).astype(bf16), _block_diag(g * m, consts),
                       preferred_element_type=f32) for t, g in zip(t_mats, g_mats)]
        t_mats = [t - tg[:CHUNK] for t, tg in zip(t_mats, tgs)]
        g_mats = [g - tg[CHUNK:] for g, tg in zip(g_mats, tgs)]
    uws = []
    for c, t in zip(chunks, t_mats):
        v_stack = stack([qkv_ref[2 * N_HEADS + h, srows(c), :] for h in heads])
        kg_stack = stack([qkv_ref[N_HEADS + h, srows(c), :]
                          * jnp.exp(gate_ref[rows(c), LANE_DECAY + h:LANE_DECAY + h + 1]) for h in heads])
        rhs = jnp.concatenate([v_stack, kg_stack], axis=1).astype(bf16)
        uws.append(jnp.dot(_block_diag(t, consts), rhs, preferred_element_type=f32).astype(bf16))
    for c, uw, qkb in zip(chunks, uws, qkbs):
        qw = jnp.dot(_block_diag(qkb, consts), uw, preferred_element_type=f32)
        for h in heads:
            hs = slice(h * CHUNK, (h + 1) * CHUNK)
            bcol = gate_ref[rows(c), LANE_BETA + h:LANE_BETA + h + 1]
            gcol = gate_ref[rows(c), LANE_DECAY + h:LANE_DECAY + h + 1]
            glast = gate_ref[(c + 1) * CHUNK - 1:(c + 1) * CHUNK, LANE_DECAY + h:LANE_DECAY + h + 1]
            kdb = qkv_ref[N_HEADS + h, srows(c), :] * (jnp.exp(glast - gcol) * bcol)
            nm = _bdot_tn(kdb, uw[hs])
            n_ref[c, h] = nm[:, :HEAD_DIM]
            oi_ref[c, h] = qw[hs, :HEAD_DIM]
            q_eff = qkv_ref[h, srows(c), :] * jnp.exp(gcol) - qw[hs, HEAD_DIM:]
            mq_ref[c, h] = jnp.concatenate([nm[:, HEAD_DIM:], q_eff], axis=0).astype(bf16)
            dec_ref[c, h] = jnp.broadcast_to(jnp.exp(glast), (8, HEAD_DIM))


def _block_kernel(x_ref, norm_w_ref, w_in_ref, pool_w_ref, pool_scale_ref,
                  conv_w_ref, a_log_ref, dt_bias_ref, dn_norm_w_ref, w_out_ref, fnorm_w_ref,
                  out_ref,
                  w_all_ref, wo_ref, pw_ref,
                  pu_ref, pz_ref, pq_ref, pk_ref, pv_ref, pdz_ref,
                  ps_ref, qkv_ref, gate_ref, gatet_ref, y_ref, s_ref,
                  mq_ref, n_ref, oi_ref, dec_ref):
    sblk = pl.program_id(1)
    first_step = (pl.program_id(0) == 0) & (sblk == 0)
    group_refs = (pu_ref, pz_ref, pq_ref, pk_ref, pv_ref, pdz_ref)

    def pslab(slab):
        return group_refs[slab // GROUP_SLABS], slab % GROUP_SLABS

    @pl.when(first_step)
    def _():
        w_all_ref[:, D_MAIN:] = jnp.zeros((D_MODEL, TAIL), jnp.bfloat16)
        for r0 in range(0, D_MODEL, W_STAGE_ROWS):
            rs = slice(r0, r0 + W_STAGE_ROWS)
            w_all_ref[rs, 0:D_IN] = w_in_ref[0, rs, :].astype(jnp.bfloat16)
            wo_ref[rs, :] = w_out_ref[0, rs, :].astype(jnp.bfloat16)
        pw_ref[...] = jnp.zeros_like(pw_ref)
        for gi in range(len(POOL_WINDOWS)):
            d0 = (gi % 2) * POOL_GROUP
            pw_ref[gi // 2, d0:d0 + POOL_GROUP, d0:d0 + POOL_GROUP] = pool_w_ref[0, gi].astype(jnp.bfloat16)
        ps_ref[:, :, 0:HALO - POOL_EXT, :] = jnp.zeros((2, 2, HALO - POOL_EXT, 128), jnp.float32)

    @pl.when(sblk == 0)
    def _():
        for ref in group_refs:
            for rb in range(ROWS):
                ref[:, rb * SEG:rb * SEG + HALO, :] = jnp.zeros((GROUP_SLABS, HALO, 128), jnp.float32)
        s_ref[...] = jnp.zeros_like(s_ref)

    @pl.when(sblk != 0)
    def _():
        for ref in group_refs:
            for rb in range(ROWS):
                ref[:, rb * SEG:rb * SEG + HALO, :] = ref[:, rb * SEG + TB:(rb + 1) * SEG, :]

    x = x_ref[...].reshape(TS, D_MODEL)
    n = x * lax.rsqrt(jnp.mean(x * x, axis=-1, keepdims=True) + NORM_EPS) * norm_w_ref[...]
    nb = n.astype(jnp.bfloat16)

    res = jnp.dot(nb, w_all_ref[:, D_MAIN - PROJ_COLS:], preferred_element_type=jnp.float32)
    _to_slab(pdz_ref, (GROUP_SLABS - 2,), res[:, :128])
    _to_slab(pdz_ref, (GROUP_SLABS - 1,), res[:, 128:256])
    tail = res[:, 256:]

    gdec = -jnp.exp(a_log_ref[...]) * jax.nn.softplus(tail + dt_bias_ref[...])
    r_in_chunk = lax.broadcasted_iota(jnp.int32, (TS, TAIL), 0) % CHUNK
    gc = gdec
    k = 1
    while k < CHUNK:
        gc = gc + jnp.where(r_in_chunk >= k, pltpu.roll(gc, k, axis=0), 0.0)
        k *= 2
    lane_t = lax.broadcasted_iota(jnp.int32, (TS, TAIL), 1)
    gates = jnp.where(lane_t < LANE_DECAY, jax.nn.sigmoid(tail), gc)
    gate_ref[...] = gates
    gatet_ref[...] = gates.T

    def proj_pair(slab):
        def run():
            c0 = slab * 128
            res = jnp.dot(nb, w_all_ref[:, c0:c0 + PROJ_COLS], preferred_element_type=jnp.float32)
            ref, j = pslab(slab)
            _to_slab(ref, (j,), res[:, :128])
            _to_slab(ref, (j + 1,), res[:, 128:])
        return run

    def taps(ref, slab, first_row, n_rows):
        lead = slab if isinstance(slab, tuple) else (slab,)
        return ref[lead + (pl.ds(first_row, n_rows, stride=2), slice(None))]

    t_pos = sblk * TB + lax.broadcasted_iota(jnp.int32, (TS, 1), 0) % TB

    def window_sum(gi, w):
        n_half = (EXT - (HALO - POOL_EXT)) // 2
        src, src_slab = pu_ref, (gi,)
        k = 1
        level = 0
        while k < w:
            for par in range(2):
                first = HALO - POOL_EXT + par
                ssum = taps(src, src_slab, first, n_half) + taps(src, src_slab, first - k, n_half)
                ps_ref[gi % 2, level % 2, pl.ds(first, n_half, stride=2), :] = ssum
            src, src_slab = ps_ref, (gi % 2, level % 2)
            k *= 2
            level += 1
        return _from_slab(src, src_slab)

    def pool_pair(pair):
        def run():
            mixes = []
            for gi in (2 * pair, 2 * pair + 1):
                w = POOL_WINDOWS[gi]
                inv_cnt = 1.0 / jnp.minimum(t_pos + 1, w).astype(jnp.float32)
                mixes.append(window_sum(gi, w) * inv_cnt - _from_slab(pu_ref, (gi,)))
            mixed = _bdot(jnp.concatenate(mixes, axis=1), pw_ref[pair])
            z = jnp.concatenate([_from_slab(pz_ref, (gi,)) for gi in (2 * pair, 2 * pair + 1)], axis=1)
            c0 = OFF_PU + 2 * pair * POOL_GROUP
            y_ref[:, c0:c0 + 2 * POOL_GROUP] = (
                mixed * pool_scale_ref[:, c0:c0 + 2 * POOL_GROUP] * _silu(z)).astype(jnp.bfloat16)
        return run

    def conv_group(ci):
        def run():
            slab = OFF_Q // 128 + ci
            cw = conv_w_ref[:, ci * HEAD_DIM:(ci + 1) * HEAD_DIM]
            n_half = (EXT - HALO) // 2
            stream = {d: taps(*pslab(slab), HALO + d, n_half) for d in range(1 - CONV_WIDTH, 2)}
            for par in range(2):
                acc = None
                for j in range(CONV_WIDTH):
                    term = stream[par - (CONV_WIDTH - 1) + j] * cw[j:j + 1, :]
                    acc = term if acc is None else acc + term
                a = _silu(acc)
                if ci < 2 * N_HEADS:
                    a = a * lax.rsqrt(jnp.sum(a * a, axis=-1, keepdims=True) + NORM_EPS)
                if ci < N_HEADS:
                    a = a * (HEAD_DIM ** -0.5)
                qkv_ref[ci, pl.ds(HALO + par, (EXT - HALO) // 2, stride=2), :] = a
        return run

    convs = [conv_group(ci) for ci in range(3 * N_HEADS)]
    pools = [None, pool_pair(0), None, pool_pair(1)]
    first_slab = {name: off // 128 for name, off in
                  dict(q=OFF_Q, k=OFF_K, v=OFF_V, pu=OFF_PU, pz=OFF_PZ, dz=OFF_DZ).items()}
    pair_order = [first_slab[name] + j for name in ("q", "k", "v", "pu", "pz", "dz") for j in (0, 2)][:-1]
    mixers = [None] * N_HEADS + convs + [None] * N_HEADS + pools
    for i, slab in enumerate(pair_order):
        proj_pair(slab)()
        for task in mixers[2 * i:2 * i + 2]:
            if task is not None:
                task()
    for task in mixers[2 * len(pair_order):]:
        if task is not None:
            task()
    consts = _packed_constants()

    def recurrence_chunk(c):
        def run():
            r0 = c * CHUNK
            for h in range(N_HEADS):
                hs = (c // ROW_CHUNKS) * N_HEADS + h
                state = s_ref[hs]
                r = jnp.dot(mq_ref[c, h], state.astype(jnp.bfloat16), preferred_element_type=jnp.float32)
                s_ref[hs] = state * dec_ref[c, h][0:1, :] + n_ref[c, h] - r[:HEAD_DIM]
                o = r[HEAD_DIM:] + oi_ref[c, h]
                o = o * lax.rsqrt(jnp.mean(o * o, axis=-1, keepdims=True) + NORM_EPS) * dn_norm_w_ref[...]
                dz = pdz_ref[h, _slab_row(r0):_slab_row(r0) + CHUNK, :]
                y_ref[r0:r0 + CHUNK, D_POOL + h * HEAD_DIM:D_POOL + (h + 1) * HEAD_DIM] = (
                    o * _silu(dz)).astype(jnp.bfloat16)
        return run

    def output_block():
        hres = x_ref[...].reshape(TS, D_MODEL) + jnp.dot(y_ref[...], wo_ref[...],
                                                         preferred_element_type=jnp.float32)
        out_ref[...] = (hres * lax.rsqrt(jnp.mean(hres * hres, axis=-1, keepdims=True) + NORM_EPS)
                        * fnorm_w_ref[...]).reshape(ROWS, TB, D_MODEL)

    _delta_precompute(range(N_CHUNKS), qkv_ref, gate_ref, gatet_ref,
                      mq_ref, n_ref, oi_ref, dec_ref, consts)
    for cc in range(ROW_CHUNKS):
        for rb in range(ROWS):
            recurrence_chunk(rb * ROW_CHUNKS + cc)()
    output_block()


def _lane_row(vals, lane0):
    return jnp.zeros((1, TAIL), jnp.float32).at[0, lane0:lane0 + vals.shape[0]].set(vals)


def kernel(x, norm_w, w_in, pool_w, pool_scale, conv_w, a_log, dt_bias, dn_norm_w, w_out, final_norm_w):
    B, S, D = x.shape
    assert D == D_MODEL and S % TB == 0 and B % ROWS == 0 and norm_w.shape[0] == 1
    assert w_in.shape == (1, D_MODEL, D_IN)

    def const(shape):
        return pl.BlockSpec(shape, lambda b, s: (0,) * len(shape), pipeline_mode=pl.Buffered(1))

    blk = pl.BlockSpec((ROWS, TB, D_MODEL), lambda b, s: (b, s, 0))
    n_groups = len(POOL_WINDOWS)
    return pl.pallas_call(
        _block_kernel,
        grid=(B // ROWS, S // TB),
        in_specs=[
            blk,
            const((1, D_MODEL)),
            const((1, D_MODEL, D_IN)),
            const((1, n_groups, POOL_GROUP, POOL_GROUP)),
            const((1, D_POOL)),
            const((CONV_WIDTH, 3 * D_DN)),
            const((1, TAIL)),
            const((1, TAIL)),
            const((1, HEAD_DIM)),
            const((1, D_MODEL, D_MODEL)),
            const((1, D_MODEL)),
        ],
        out_specs=blk,
        out_shape=jax.ShapeDtypeStruct(x.shape, x.dtype),
        scratch_shapes=[
            pltpu.VMEM((D_MODEL, D_MAIN + TAIL), jnp.bfloat16),
            pltpu.VMEM((D_MODEL, D_MODEL), jnp.bfloat16),
            pltpu.VMEM((n_groups // 2, 2 * POOL_GROUP, 2 * POOL_GROUP), jnp.bfloat16),
        ] + [pltpu.VMEM((GROUP_SLABS, EXT, 128), jnp.float32)
             for _ in range(N_SLABS // GROUP_SLABS)] + [
            pltpu.VMEM((2, 2, EXT, 128), jnp.float32),
            pltpu.VMEM((3 * N_HEADS, EXT, HEAD_DIM), jnp.float32),
            pltpu.VMEM((TS, TAIL), jnp.float32),
            pltpu.VMEM((TAIL, TS), jnp.float32),
            pltpu.VMEM((TS, D_MODEL), jnp.bfloat16),
            pltpu.VMEM((ROWS * N_HEADS, HEAD_DIM, HEAD_DIM), jnp.float32),
            pltpu.VMEM((N_CHUNKS, N_HEADS, HEAD_DIM + CHUNK, HEAD_DIM), jnp.bfloat16),
            pltpu.VMEM((N_CHUNKS, N_HEADS, HEAD_DIM, HEAD_DIM), jnp.float32),
            pltpu.VMEM((N_CHUNKS, N_HEADS, CHUNK, HEAD_DIM), jnp.float32),
            pltpu.VMEM((N_CHUNKS, N_HEADS, 8, HEAD_DIM), jnp.float32),
        ],
        compiler_params=pltpu.CompilerParams(
            dimension_semantics=("arbitrary", "arbitrary"),
            vmem_limit_bytes=VMEM_LIMIT_BYTES),
        name="hymba_block",
    )(x, norm_w, w_in, pool_w, pool_scale, conv_w[0],
      _lane_row(a_log[0], LANE_DECAY), _lane_row(dt_bias[0], LANE_DECAY),
      dn_norm_w, w_out, final_norm_w[None, :])
```

```python
import jax
import jax.numpy as jnp
from jax import lax
from jax.experimental import pallas as pl
from jax.experimental.pallas import tpu as pltpu

D_MODEL = 1024
D_POOL = 512
D_DN = 512
POOL_WINDOWS = (2, 4, 8, 16)
POOL_GROUP = 128
HEAD_DIM = 128
N_HEADS = 4
CONV_WIDTH = 4
NORM_EPS = 1e-6

D_MAIN = 2 * D_POOL + 4 * D_DN
D_IN = D_MAIN + 2 * N_HEADS
TAIL = 128
OFF_PU, OFF_PZ, OFF_Q, OFF_K, OFF_V, OFF_DZ = 0, 512, 1024, 1536, 2048, 2560
LANE_BETA, LANE_DECAY = 0, N_HEADS

HALO = 24
POOL_EXT = 16
N_SLABS = (2 * D_POOL + 4 * D_DN) // 128
GROUP_SLABS = 4
CHUNK = 64
TS = 512
ROWS = 4
TB = TS // ROWS
SEG = HALO + TB
EXT = ROWS * SEG
N_CHUNKS = TS // CHUNK
ROW_CHUNKS = TB // CHUNK
PROJ_COLS = 256
W_STAGE_ROWS = 128

VMEM_LIMIT_BYTES = 56 * 1024 * 1024


def _bdot(a, b):
    return jnp.dot(a.astype(jnp.bfloat16), b.astype(jnp.bfloat16),
                   preferred_element_type=jnp.float32)


def _bdot_tn(a, b):
    return lax.dot_general(a.astype(jnp.bfloat16), b.astype(jnp.bfloat16),
                           (((0,), (0,)), ((), ())), preferred_element_type=jnp.float32)


def _slab_row(r):
    return (r // TB) * SEG + HALO + r % TB


def _from_slab(ref, lead):
    return jnp.concatenate([ref[lead + (slice(rb * SEG + HALO, (rb + 1) * SEG), slice(None))]
                            for rb in range(ROWS)], axis=0)


def _to_slab(ref, lead, val):
    for rb in range(ROWS):
        ref[lead + (slice(rb * SEG + HALO, (rb + 1) * SEG), slice(None))] = val[rb * TB:(rb + 1) * TB]


def _silu(x):
    h = 0.5 * x
    return h + h * jnp.tanh(h)


def _packed_constants():
    shape = (CHUNK, N_HEADS * CHUNK)
    row = lax.broadcasted_iota(jnp.int32, shape, 0)
    lane = lax.broadcasted_iota(jnp.int32, shape, 1)
    col = lane % CHUNK
    blk = lane // CHUNK
    f32 = jnp.float32
    levels = []
    r = 1
    while r < CHUNK:
        levels.append((((row // (2 * r)) == (col // (2 * r)))
                       & ((row % (2 * r)) >= r) & ((col % (2 * r)) < r)).astype(f32))
        r *= 2
    return dict(
        causal=row >= col,
        strict=(row > col).astype(f32),
        eye=(row == col).astype(f32),
        levels=levels,
        blk=blk,
        head=[(blk == h).astype(jnp.bfloat16) for h in range(N_HEADS)],
    )


def _block_diag(p, consts):
    pb = p.astype(jnp.bfloat16)
    return jnp.concatenate([pb * m for m in consts["head"]], axis=0)


def _delta_precompute(chunks, qkv_ref, gate_ref, gatet_ref, mq_ref, n_ref, oi_ref, dec_ref, consts):
    f32, bf16 = jnp.float32, jnp.bfloat16
    heads = range(N_HEADS)

    def rows(c):
        return slice(c * CHUNK, (c + 1) * CHUNK)

    def srows(c):
        start = _slab_row(c * CHUNK)
        return slice(start, start + CHUNK)

    def lane_row(c, first):
        return jnp.concatenate([gatet_ref[first + h:first + h + 1, rows(c)] for h in heads], axis=1)

    def stack(parts):
        return jnp.concatenate(parts, axis=0)

    a_mats, t_mats, qkbs = [], [], []
    for c in chunks:
        brow = lane_row(c, LANE_BETA)
        grow = lane_row(c, LANE_DECAY)
        gcol = None
        for h in reversed(heads):
            bc = jnp.broadcast_to(gate_ref[rows(c), LANE_DECAY + h:LANE_DECAY + h + 1],
                                  (CHUNK, N_HEADS * CHUNK))
            gcol = bc if gcol is None else jnp.where(consts["blk"] == h, bc, gcol)
        db = jnp.exp(jnp.where(consts["causal"], gcol - grow, -jnp.inf)) * brow
        k16 = [qkv_ref[N_HEADS + h, srows(c), :].astype(bf16) for h in heads]
        q16 = [qkv_ref[h, srows(c), :].astype(bf16) for h in heads]
        zero = jnp.zeros((CHUNK, HEAD_DIM), bf16)
        kdiag = jnp.concatenate(
            [jnp.concatenate([k16[h] if j == h else zero for j in heads], axis=1) for h in heads], axis=0)
        lhs = jnp.concatenate([jnp.concatenate(k16, axis=1), jnp.concatenate(q16, axis=1)], axis=0)
        kq = lax.dot_general(lhs, kdiag, (((1,), (1,)), ((), ())), preferred_element_type=f32)
        a_mat = kq[:CHUNK] * db * consts["strict"]
        a_mats.append(a_mat)
        qkbs.append(kq[CHUNK:] * db)
        t_mats.append(consts["eye"] - a_mat * consts["levels"][0])
    g_mats = [a - jnp.dot(a.astype(bf16), _block_diag(a * consts["levels"][0], consts),
                          preferred_element_type=f32) for a in a_mats]
    for m in consts["levels"][1:]:
        tgs = [jnp.dot(jnp.concatenate([t, g], axis=0).astype(bf16), _block_diag(g * m, consts),
                       preferred_element_type=f32) for t, g in zip(t_mats, g_mats)]
        t_mats = [t - tg[:CHUNK] for t, tg in zip(t_mats, tgs)]
        g_mats = [g - tg[CHUNK:] for g, tg in zip(g_mats, tgs)]
    uws = []
    for c, t in zip(chunks, t_mats):
        v_stack = stack([qkv_ref[2 * N_HEADS + h, srows(c), :] for h in heads])
        kg_stack = stack([qkv_ref[N_HEADS + h, srows(c), :]
                          * jnp.exp(gate_ref[rows(c), LANE_DECAY + h:LANE_DECAY + h + 1]) for h in heads])
        rhs = jnp.concatenate([v_stack, kg_stack], axis=1).astype(bf16)
        uws.append(jnp.dot(_block_diag(t, consts), rhs, preferred_element_type=f32).astype(bf16))
    for c, uw, qkb in zip(chunks, uws, qkbs):
        k_t = stack([qkv_ref[N_HEADS + h, srows(c), :] for h in heads]).T
        glast_row = jnp.concatenate(
            [jnp.broadcast_to(gatet_ref[LANE_DECAY + h:LANE_DECAY + h + 1, (c + 1) * CHUNK - 1:(c + 1) * CHUNK],
                              (1, CHUNK)) for h in heads], axis=1)
        kdb_t = (k_t * (jnp.exp(glast_row - lane_row(c, LANE_DECAY)) * lane_row(c, LANE_BETA))).astype(bf16)
        lhs = jnp.concatenate([_block_diag(qkb, consts)] + [kdb_t * m[0:1, :] for m in consts["head"]], axis=0)
        prod = jnp.dot(lhs, uw, preferred_element_type=f32)
        qw, nm_all = prod[:N_HEADS * CHUNK], prod[N_HEADS * CHUNK:]
        for h in heads:
            hs = slice(h * CHUNK, (h + 1) * CHUNK)
            gcol = gate_ref[rows(c), LANE_DECAY + h:LANE_DECAY + h + 1]
            glast = gate_ref[(c + 1) * CHUNK - 1:(c + 1) * CHUNK, LANE_DECAY + h:LANE_DECAY + h + 1]
            nm = nm_all[h * HEAD_DIM:(h + 1) * HEAD_DIM]
            n_ref[c, h] = nm[:, :HEAD_DIM]
            oi_ref[c, h] = qw[hs, :HEAD_DIM]
            q_eff = qkv_ref[h, srows(c), :] * jnp.exp(gcol) - qw[hs, HEAD_DIM:]
            mq_ref[c, h] = jnp.concatenate([nm[:, HEAD_DIM:], q_eff], axis=0).astype(bf16)
            dec_ref[c, h] = jnp.broadcast_to(jnp.exp(glast), (8, HEAD_DIM))


def _block_kernel(x_ref, norm_w_ref, w_in_ref, pool_w_ref, pool_scale_ref,
                  conv_w_ref, a_log_ref, dt_bias_ref, dn_norm_w_ref, w_out_ref, fnorm_w_ref,
                  out_ref,
                  w_all_ref, wo_ref, pw_ref,
                  pu_ref, pz_ref, pq_ref, pk_ref, pv_ref, pdz_ref,
                  ps_ref, qkv_ref, gate_ref, gatet_ref, y_ref, s_ref,
                  mq_ref, n_ref, oi_ref, dec_ref):
    sblk = pl.program_id(1)
    first_step = (pl.program_id(0) == 0) & (sblk == 0)
    group_refs = (pu_ref, pz_ref, pq_ref, pk_ref, pv_ref, pdz_ref)

    def pslab(slab):
        return group_refs[slab // GROUP_SLABS], slab % GROUP_SLABS

    @pl.when(first_step)
    def _():
        w_all_ref[:, D_MAIN:] = jnp.zeros((D_MODEL, TAIL), jnp.bfloat16)
        for r0 in range(0, D_MODEL, W_STAGE_ROWS):
            rs = slice(r0, r0 + W_STAGE_ROWS)
            w_all_ref[rs, 0:D_IN] = w_in_ref[0, rs, :].astype(jnp.bfloat16)
            wo_ref[rs, :] = w_out_ref[0, rs, :].astype(jnp.bfloat16)
        pw_ref[...] = jnp.zeros_like(pw_ref)
        for gi in range(len(POOL_WINDOWS)):
            d0 = (gi % 2) * POOL_GROUP
            pw_ref[gi // 2, d0:d0 + POOL_GROUP, d0:d0 + POOL_GROUP] = pool_w_ref[0, gi].astype(jnp.bfloat16)
        ps_ref[:, :, 0:HALO - POOL_EXT, :] = jnp.zeros((2, 2, HALO - POOL_EXT, 128), jnp.float32)

    @pl.when(sblk == 0)
    def _():
        for ref in group_refs:
            for rb in range(ROWS):
                ref[:, rb * SEG:rb * SEG + HALO, :] = jnp.zeros((GROUP_SLABS, HALO, 128), jnp.float32)
        s_ref[...] = jnp.zeros_like(s_ref)

    @pl.when(sblk != 0)
    def _():
        for ref in group_refs:
            for rb in range(ROWS):
                ref[:, rb * SEG:rb * SEG + HALO, :] = ref[:, rb * SEG + TB:(rb + 1) * SEG, :]

    x = x_ref[...].reshape(TS, D_MODEL)
    n = x * lax.rsqrt(jnp.mean(x * x, axis=-1, keepdims=True) + NORM_EPS) * norm_w_ref[...]
    nb = n.astype(jnp.bfloat16)

    res = jnp.dot(nb, w_all_ref[:, D_MAIN - PROJ_COLS:], preferred_element_type=jnp.float32)
    _to_slab(pdz_ref, (GROUP_SLABS - 2,), res[:, :128])
    _to_slab(pdz_ref, (GROUP_SLABS - 1,), res[:, 128:256])
    tail = res[:, 256:]

    gdec = -jnp.exp(a_log_ref[...]) * jax.nn.softplus(tail + dt_bias_ref[...])
    r_in_chunk = lax.broadcasted_iota(jnp.int32, (TS, TAIL), 0) % CHUNK
    gc = gdec
    k = 1
    while k < CHUNK:
        gc = gc + jnp.where(r_in_chunk >= k, pltpu.roll(gc, k, axis=0), 0.0)
        k *= 2
    lane_t = lax.broadcasted_iota(jnp.int32, (TS, TAIL), 1)
    gates = jnp.where(lane_t < LANE_DECAY, jax.nn.sigmoid(tail), gc)
    gate_ref[...] = gates
    gatet_ref[...] = gates.T

    def proj_pair(slab):
        def run():
            c0 = slab * 128
            res = jnp.dot(nb, w_all_ref[:, c0:c0 + PROJ_COLS], preferred_element_type=jnp.float32)
            ref, j = pslab(slab)
            _to_slab(ref, (j,), res[:, :128])
            _to_slab(ref, (j + 1,), res[:, 128:])
        return run

    def taps(ref, slab, first_row, n_rows):
        lead = slab if isinstance(slab, tuple) else (slab,)
        return ref[lead + (pl.ds(first_row, n_rows, stride=2), slice(None))]

    t_pos = sblk * TB + lax.broadcasted_iota(jnp.int32, (TS, 1), 0) % TB

    def window_sum(gi, w):
        n_half = (EXT - (HALO - POOL_EXT)) // 2
        src, src_slab = pu_ref, (gi,)
        k = 1
        level = 0
        while k < w:
            for par in range(2):
                first = HALO - POOL_EXT + par
                ssum = taps(src, src_slab, first, n_half) + taps(src, src_slab, first - k, n_half)
                ps_ref[gi % 2, level % 2, pl.ds(first, n_half, stride=2), :] = ssum
            src, src_slab = ps_ref, (gi % 2, level % 2)
            k *= 2
            level += 1
        return _from_slab(src, src_slab)

    def pool_pair(pair):
        def run():
            mixes = []
            for gi in (2 * pair, 2 * pair + 1):
                w = POOL_WINDOWS[gi]
                inv_cnt = 1.0 / jnp.minimum(t_pos + 1, w).astype(jnp.float32)
                mixes.append(window_sum(gi, w) * inv_cnt - _from_slab(pu_ref, (gi,)))
            mixed = _bdot(jnp.concatenate(mixes, axis=1), pw_ref[pair])
            z = jnp.concatenate([_from_slab(pz_ref, (gi,)) for gi in (2 * pair, 2 * pair + 1)], axis=1)
            c0 = OFF_PU + 2 * pair * POOL_GROUP
            y_ref[:, c0:c0 + 2 * POOL_GROUP] = (
                mixed * pool_scale_ref[:, c0:c0 + 2 * POOL_GROUP] * _silu(z)).astype(jnp.bfloat16)
        return run

    def conv_group(ci):
        def run():
            slab = OFF_Q // 128 + ci
            cw = conv_w_ref[:, ci * HEAD_DIM:(ci + 1) * HEAD_DIM]
            n_half = (EXT - HALO) // 2
            stream = {d: taps(*pslab(slab), HALO + d, n_half) for d in range(1 - CONV_WIDTH, 2)}
            for par in range(2):
                acc = None
                for j in range(CONV_WIDTH):
                    term = stream[par - (CONV_WIDTH - 1) + j] * cw[j:j + 1, :]
                    acc = term if acc is None else acc + term
                a = _silu(acc)
                if ci < 2 * N_HEADS:
                    a = a * lax.rsqrt(jnp.sum(a * a, axis=-1, keepdims=True) + NORM_EPS)
                if ci < N_HEADS:
                    a = a * (HEAD_DIM ** -0.5)
                qkv_ref[ci, pl.ds(HALO + par, (EXT - HALO) // 2, stride=2), :] = a
        return run

    convs = [conv_group(ci) for ci in range(3 * N_HEADS)]
    pools = [None, pool_pair(0), None, pool_pair(1)]
    first_slab = {name: off // 128 for name, off in
                  dict(q=OFF_Q, k=OFF_K, v=OFF_V, pu=OFF_PU, pz=OFF_PZ, dz=OFF_DZ).items()}
    pair_order = [first_slab[name] + j for name in ("q", "k", "v", "pu", "pz", "dz") for j in (0, 2)][:-1]
    mixers = [None] * N_HEADS + convs + [None] * N_HEADS + pools
    for i, slab in enumerate(pair_order):
        proj_pair(slab)()
        for task in mixers[2 * i:2 * i + 2]:
            if task is not None:
                task()
    for task in mixers[2 * len(pair_order):]:
        if task is not None:
            task()
    consts = _packed_constants()

    def recurrence_chunk(c):
        def run():
            r0 = c * CHUNK
            for h in range(N_HEADS):
                hs = (c // ROW_CHUNKS) * N_HEADS + h
                state = s_ref[hs]
                r = jnp.dot(mq_ref[c, h], state.astype(jnp.bfloat16), preferred_element_type=jnp.float32)
                s_ref[hs] = state * dec_ref[c, h][0:1, :] + n_ref[c, h] - r[:HEAD_DIM]
                o = r[HEAD_DIM:] + oi_ref[c, h]
                o = o * lax.rsqrt(jnp.mean(o * o, axis=-1, keepdims=True) + NORM_EPS) * dn_norm_w_ref[...]
                dz = pdz_ref[h, _slab_row(r0):_slab_row(r0) + CHUNK, :]
                y_ref[r0:r0 + CHUNK, D_POOL + h * HEAD_DIM:D_POOL + (h + 1) * HEAD_DIM] = (
                    o * _silu(dz)).astype(jnp.bfloat16)
        return run

    def output_block():
        hres = x_ref[...].reshape(TS, D_MODEL) + jnp.dot(y_ref[...], wo_ref[...],
                                                         preferred_element_type=jnp.float32)
        out_ref[...] = (hres * lax.rsqrt(jnp.mean(hres * hres, axis=-1, keepdims=True) + NORM_EPS)
                        * fnorm_w_ref[...]).reshape(ROWS, TB, D_MODEL)

    _delta_precompute(range(N_CHUNKS), qkv_ref, gate_ref, gatet_ref,
                      mq_ref, n_ref, oi_ref, dec_ref, consts)
    for cc in range(ROW_CHUNKS):
        for rb in range(ROWS):
            recurrence_chunk(rb * ROW_CHUNKS + cc)()
    output_block()


def _lane_row(vals, lane0):
    return jnp.zeros((1, TAIL), jnp.float32).at[0, lane0:lane0 + vals.shape[0]].set(vals)


def kernel(x, norm_w, w_in, pool_w, pool_scale, conv_w, a_log, dt_bias, dn_norm_w, w_out, final_norm_w):
    B, S, D = x.shape
    assert D == D_MODEL and S % TB == 0 and B % ROWS == 0 and norm_w.shape[0] == 1
    assert w_in.shape == (1, D_MODEL, D_IN)

    def const(shape):
        return pl.BlockSpec(shape, lambda b, s: (0,) * len(shape), pipeline_mode=pl.Buffered(1))

    blk = pl.BlockSpec((ROWS, TB, D_MODEL), lambda b, s: (b, s, 0))
    n_groups = len(POOL_WINDOWS)
    return pl.pallas_call(
        _block_kernel,
        grid=(B // ROWS, S // TB),
        in_specs=[
            blk,
            const((1, D_MODEL)),
            const((1, D_MODEL, D_IN)),
            const((1, n_groups, POOL_GROUP, POOL_GROUP)),
            const((1, D_POOL)),
            const((CONV_WIDTH, 3 * D_DN)),
            const((1, TAIL)),
            const((1, TAIL)),
            const((1, HEAD_DIM)),
            const((1, D_MODEL, D_MODEL)),
            const((1, D_MODEL)),
        ],
        out_specs=blk,
        out_shape=jax.ShapeDtypeStruct(x.shape, x.dtype),
        scratch_shapes=[
            pltpu.VMEM((D_MODEL, D_MAIN + TAIL), jnp.bfloat16),
            pltpu.VMEM((D_MODEL, D_MODEL), jnp.bfloat16),
            pltpu.VMEM((n_groups // 2, 2 * POOL_GROUP, 2 * POOL_GROUP), jnp.bfloat16),
        ] + [pltpu.VMEM((GROUP_SLABS, EXT, 128), jnp.float32)
             for _ in range(N_SLABS // GROUP_SLABS)] + [
            pltpu.VMEM((2, 2, EXT, 128), jnp.float32),
            pltpu.VMEM((3 * N_HEADS, EXT, HEAD_DIM), jnp.float32),
            pltpu.VMEM((TS, TAIL), jnp.float32),
            pltpu.VMEM((TAIL, TS), jnp.float32),
            pltpu.VMEM((TS, D_MODEL), jnp.bfloat16),
            pltpu.VMEM((ROWS * N_HEADS, HEAD_DIM, HEAD_DIM), jnp.float32),
            pltpu.VMEM((N_CHUNKS, N_HEADS, HEAD_DIM + CHUNK, HEAD_DIM), jnp.bfloat16),
            pltpu.VMEM((N_CHUNKS, N_HEADS, HEAD_DIM, HEAD_DIM), jnp.float32),
            pltpu.VMEM((N_CHUNKS, N_HEADS, CHUNK, HEAD_DIM), jnp.float32),
            pltpu.VMEM((N_CHUNKS, N_HEADS, 8, HEAD_DIM), jnp.float32),
        ],
        compiler_params=pltpu.CompilerParams(
            dimension_semantics=("arbitrary", "arbitrary"),
            vmem_limit_bytes=VMEM_LIMIT_BYTES),
        name="hymba_block",
    )(x, norm_w, w_in, pool_w, pool_scale, conv_w[0],
      _lane_row(a_log[0], LANE_DECAY), _lane_row(dt_bias[0], LANE_DECAY),
      dn_norm_w, w_out, final_norm_w[None, :])
```

```python
import jax
import jax.numpy as jnp
from jax import lax
from jax.experimental import pallas as pl
from jax.experimental.pallas import tpu as pltpu

D_MODEL = 1024
D_POOL = 512
D_DN = 512
POOL_WINDOWS = (2, 4, 8, 16)
POOL_GROUP = 128
HEAD_DIM = 128
N_HEADS = 4
CONV_WIDTH = 4
NORM_EPS = 1e-6

D_MAIN = 2 * D_POOL + 4 * D_DN
D_IN = D_MAIN + 2 * N_HEADS
TAIL = 128
OFF_PU, OFF_PZ, OFF_Q, OFF_K, OFF_V, OFF_DZ = 0, 512, 1024, 1536, 2048, 2560
LANE_BETA, LANE_DECAY = 0, N_HEADS

HALO = 24
POOL_EXT = 16
N_SLABS = (2 * D_POOL + 4 * D_DN) // 128
GROUP_SLABS = 4
CHUNK = 64
TS = 512
ROWS = 2
TB = TS // ROWS
SEG = HALO + TB
EXT = ROWS * SEG
N_CHUNKS = TS // CHUNK
ROW_CHUNKS = TB // CHUNK
PROJ_COLS = 256
W_STAGE_ROWS = 128

VMEM_LIMIT_BYTES = 56 * 1024 * 1024


def _bdot(a, b):
    return jnp.dot(a.astype(jnp.bfloat16), b.astype(jnp.bfloat16),
                   preferred_element_type=jnp.float32)


def _bdot_tn(a, b):
    return lax.dot_general(a.astype(jnp.bfloat16), b.astype(jnp.bfloat16),
                           (((0,), (0,)), ((), ())), preferred_element_type=jnp.float32)


def _slab_row(r):
    return (r // TB) * SEG + HALO + r % TB


def _from_slab(ref, lead):
    return jnp.concatenate([ref[lead + (slice(rb * SEG + HALO, (rb + 1) * SEG), slice(None))]
                            for rb in range(ROWS)], axis=0)


def _to_slab(ref, lead, val):
    for rb in range(ROWS):
        ref[lead + (slice(rb * SEG + HALO, (rb + 1) * SEG), slice(None))] = val[rb * TB:(rb + 1) * TB]


def _silu(x):
    h = 0.5 * x
    return h + h * jnp.tanh(h)


def _packed_constants():
    shape = (CHUNK, N_HEADS * CHUNK)
    row = lax.broadcasted_iota(jnp.int32, shape, 0)
    lane = lax.broadcasted_iota(jnp.int32, shape, 1)
    col = lane % CHUNK
    blk = lane // CHUNK
    f32 = jnp.float32
    levels = []
    r = 1
    while r < CHUNK:
        levels.append((((row // (2 * r)) == (col // (2 * r)))
                       & ((row % (2 * r)) >= r) & ((col % (2 * r)) < r)).astype(f32))
        r *= 2
    return dict(
        causal=row >= col,
        strict=(row > col).astype(f32),
        eye=(row == col).astype(f32),
        levels=levels,
        blk=blk,
        head=[(blk == h).astype(jnp.bfloat16) for h in range(N_HEADS)],
    )


def _block_diag(p, consts):
    pb = p.astype(jnp.bfloat16)
    return jnp.concatenate([pb * m for m in consts["head"]], axis=0)


def _delta_precompute(chunks, qkv_ref, gate_ref, gatet_ref, mq_ref, n_ref, oi_ref, dec_ref, consts):
    f32, bf16 = jnp.float32, jnp.bfloat16
    heads = range(N_HEADS)

    def rows(c):
        return slice(c * CHUNK, (c + 1) * CHUNK)

    def srows(c):
        start = _slab_row(c * CHUNK)
        return slice(start, start + CHUNK)

    def lane_row(c, first):
        return jnp.concatenate([gatet_ref[first + h:first + h + 1, rows(c)] for h in heads], axis=1)

    def stack(parts):
        return jnp.concatenate(parts, axis=0)

    a_mats, t_mats, qkbs = [], [], []
    for c in chunks:
        brow = lane_row(c, LANE_BETA)
        grow = lane_row(c, LANE_DECAY)
        gcol = None
        for h in reversed(heads):
            bc = jnp.broadcast_to(gate_ref[rows(c), LANE_DECAY + h:LANE_DECAY + h + 1],
                                  (CHUNK, N_HEADS * CHUNK))
            gcol = bc if gcol is None else jnp.where(consts["blk"] == h, bc, gcol)
        db = jnp.exp(jnp.where(consts["causal"], gcol - grow, -jnp.inf)) * brow
        k16 = [qkv_ref[N_HEADS + h, srows(c), :].astype(bf16) for h in heads]
        q16 = [qkv_ref[h, srows(c), :].astype(bf16) for h in heads]
        zero = jnp.zeros((CHUNK, HEAD_DIM), bf16)
        kdiag = jnp.concatenate(
            [jnp.concatenate([k16[h] if j == h else zero for j in heads], axis=1) for h in heads], axis=0)
        lhs = jnp.concatenate([jnp.concatenate(k16, axis=1), jnp.concatenate(q16, axis=1)], axis=0)
        kq = lax.dot_general(lhs, kdiag, (((1,), (1,)), ((), ())), preferred_element_type=f32)
        a_mat = kq[:CHUNK] * db * consts["strict"]
        a_mats.append(a_mat)
        qkbs.append(kq[CHUNK:] * db)
        t_mats.append(consts["eye"] - a_mat * consts["levels"][0])
    g_mats = [a - jnp.dot(a.astype(bf16), _block_diag(a * consts["levels"][0], consts),
                          preferred_element_type=f32) for a in a_mats]
    for m in consts["levels"][1:]:
        tgs = [jnp.dot(jnp.concatenate([t, g], axis=0).astype(bf16), _block_diag(g * m, consts),
                       preferred_element_type=f32) for t, g in zip(t_mats, g_mats)]
        t_mats = [t - tg[:CHUNK] for t, tg in zip(t_mats, tgs)]
        g_mats = [g - tg[CHUNK:] for g, tg in zip(g_mats, tgs)]
    uws = []
    for c, t in zip(chunks, t_mats):
        v_stack = stack([qkv_ref[2 * N_HEADS + h, srows(c), :] for h in heads])
        kg_stack = stack([qkv_ref[N_HEADS + h, srows(c), :]
                          * jnp.exp(gate_ref[rows(c), LANE_DECAY + h:LANE_DECAY + h + 1]) for h in heads])
        rhs = jnp.concatenate([v_stack, kg_stack], axis=1).astype(bf16)
        uws.append(jnp.dot(_block_diag(t, consts), rhs, preferred_element_type=f32).astype(bf16))
    for c, uw, qkb in zip(chunks, uws, qkbs):
        k_t = stack([qkv_ref[N_HEADS + h, srows(c), :] for h in heads]).T
        glast_row = jnp.concatenate(
            [jnp.broadcast_to(gatet_ref[LANE_DECAY + h:LANE_DECAY + h + 1, (c + 1) * CHUNK - 1:(c + 1) * CHUNK],
                              (1, CHUNK)) for h in heads], axis=1)
        kdb_t = (k_t * (jnp.exp(glast_row - lane_row(c, LANE_DECAY)) * lane_row(c, LANE_BETA))).astype(bf16)
        lhs = jnp.concatenate([_block_diag(qkb, consts)] + [kdb_t * m[0:1, :] for m in consts["head"]], axis=0)
        prod = jnp.dot(lhs, uw, preferred_element_type=f32)
        qw, nm_all = prod[:N_HEADS * CHUNK], prod[N_HEADS * CHUNK:]
        for h in heads:
            hs = slice(h * CHUNK, (h + 1) * CHUNK)
            gcol = gate_ref[rows(c), LANE_DECAY + h:LANE_DECAY + h + 1]
            glast = gate_ref[(c + 1) * CHUNK - 1:(c + 1) * CHUNK, LANE_DECAY + h:LANE_DECAY + h + 1]
            nm = nm_all[h * HEAD_DIM:(h + 1) * HEAD_DIM]
            n_ref[c, h] = nm[:, :HEAD_DIM]
            oi_ref[c, h] = qw[hs, :HEAD_DIM]
            q_eff = qkv_ref[h, srows(c), :] * jnp.exp(gcol) - qw[hs, HEAD_DIM:]
            mq_ref[c, h] = jnp.concatenate([nm[:, HEAD_DIM:], q_eff], axis=0).astype(bf16)
            dec_ref[c, h] = jnp.broadcast_to(jnp.exp(glast), (8, HEAD_DIM))


def _block_kernel(x_ref, norm_w_ref, w_in_ref, pool_w_ref, pool_scale_ref,
                  conv_w_ref, a_log_ref, dt_bias_ref, dn_norm_w_ref, w_out_ref, fnorm_w_ref,
                  out_ref,
                  w_all_ref, wo_ref, pw_ref,
                  pu_ref, pz_ref, pq_ref, pk_ref, pv_ref, pdz_ref,
                  ps_ref, qkv_ref, gate_ref, gatet_ref, y_ref, s_ref,
                  mq_ref, n_ref, oi_ref, dec_ref):
    sblk = pl.program_id(1)
    first_step = (pl.program_id(0) == 0) & (sblk == 0)
    group_refs = (pu_ref, pz_ref, pq_ref, pk_ref, pv_ref, pdz_ref)

    def pslab(slab):
        return group_refs[slab // GROUP_SLABS], slab % GROUP_SLABS

    @pl.when(first_step)
    def _():
        w_all_ref[:, D_MAIN:] = jnp.zeros((D_MODEL, TAIL), jnp.bfloat16)
        for r0 in range(0, D_MODEL, W_STAGE_ROWS):
            rs = slice(r0, r0 + W_STAGE_ROWS)
            w_all_ref[rs, 0:D_IN] = w_in_ref[0, rs, :].astype(jnp.bfloat16)
            wo_ref[rs, :] = w_out_ref[0, rs, :].astype(jnp.bfloat16)
        pw_ref[...] = jnp.zeros_like(pw_ref)
        for gi in range(len(POOL_WINDOWS)):
            d0 = (gi % 2) * POOL_GROUP
            pw_ref[gi // 2, d0:d0 + POOL_GROUP, d0:d0 + POOL_GROUP] = pool_w_ref[0, gi].astype(jnp.bfloat16)
        ps_ref[:, :, 0:HALO - POOL_EXT, :] = jnp.zeros((2, 2, HALO - POOL_EXT, 128), jnp.float32)

    @pl.when(sblk == 0)
    def _():
        for ref in group_refs:
            for rb in range(ROWS):
                ref[:, rb * SEG:rb * SEG + HALO, :] = jnp.zeros((GROUP_SLABS, HALO, 128), jnp.float32)
        s_ref[...] = jnp.zeros_like(s_ref)

    @pl.when(sblk != 0)
    def _():
        for ref in group_refs:
            for rb in range(ROWS):
                ref[:, rb * SEG:rb * SEG + HALO, :] = ref[:, rb * SEG + TB:(rb + 1) * SEG, :]

    x = x_ref[...].reshape(TS, D_MODEL)
    n = x * lax.rsqrt(jnp.mean(x * x, axis=-1, keepdims=True) + NORM_EPS) * norm_w_ref[...]
    nb = n.astype(jnp.bfloat16)

    res = jnp.dot(nb, w_all_ref[:, D_MAIN - PROJ_COLS:], preferred_element_type=jnp.float32)
    _to_slab(pdz_ref, (GROUP_SLABS - 2,), res[:, :128])
    _to_slab(pdz_ref, (GROUP_SLABS - 1,), res[:, 128:256])
    tail = res[:, 256:]

    gdec = -jnp.exp(a_log_ref[...]) * jax.nn.softplus(tail + dt_bias_ref[...])
    r_in_chunk = lax.broadcasted_iota(jnp.int32, (TS, TAIL), 0) % CHUNK
    gc = gdec
    k = 1
    while k < CHUNK:
        gc = gc + jnp.where(r_in_chunk >= k, pltpu.roll(gc, k, axis=0), 0.0)
        k *= 2
    lane_t = lax.broadcasted_iota(jnp.int32, (TS, TAIL), 1)
    gates = jnp.where(lane_t < LANE_DECAY, jax.nn.sigmoid(tail), gc)
    gate_ref[...] = gates
    gatet_ref[...] = gates.T

    def proj_pair(slab):
        def run():
            c0 = slab * 128
            res = jnp.dot(nb, w_all_ref[:, c0:c0 + PROJ_COLS], preferred_element_type=jnp.float32)
            ref, j = pslab(slab)
            _to_slab(ref, (j,), res[:, :128])
            _to_slab(ref, (j + 1,), res[:, 128:])
        return run

    def taps(ref, slab, first_row, n_rows):
        lead = slab if isinstance(slab, tuple) else (slab,)
        return ref[lead + (pl.ds(first_row, n_rows, stride=2), slice(None))]

    t_pos = sblk * TB + lax.broadcasted_iota(jnp.int32, (TS, 1), 0) % TB

    def window_sum(gi, w):
        n_half = (EXT - (HALO - POOL_EXT)) // 2
        src, src_slab = pu_ref, (gi,)
        k = 1
        level = 0
        while k < w:
            for par in range(2):
                first = HALO - POOL_EXT + par
                ssum = taps(src, src_slab, first, n_half) + taps(src, src_slab, first - k, n_half)
                ps_ref[gi % 2, level % 2, pl.ds(first, n_half, stride=2), :] = ssum
            src, src_slab = ps_ref, (gi % 2, level % 2)
            k *= 2
            level += 1
        return _from_slab(src, src_slab)

    def pool_pair(pair):
        def run():
            mixes = []
            for gi in (2 * pair, 2 * pair + 1):
                w = POOL_WINDOWS[gi]
                inv_cnt = 1.0 / jnp.minimum(t_pos + 1, w).astype(jnp.float32)
                mixes.append(window_sum(gi, w) * inv_cnt - _from_slab(pu_ref, (gi,)))
            mixed = _bdot(jnp.concatenate(mixes, axis=1), pw_ref[pair])
            z = jnp.concatenate([_from_slab(pz_ref, (gi,)) for gi in (2 * pair, 2 * pair + 1)], axis=1)
            c0 = OFF_PU + 2 * pair * POOL_GROUP
            y_ref[:, c0:c0 + 2 * POOL_GROUP] = (
                mixed * pool_scale_ref[:, c0:c0 + 2 * POOL_GROUP] * _silu(z)).astype(jnp.bfloat16)
        return run

    def conv_group(ci):
        def run():
            slab = OFF_Q // 128 + ci
            cw = conv_w_ref[:, ci * HEAD_DIM:(ci + 1) * HEAD_DIM]
            n_half = (EXT - HALO) // 2
            stream = {d: taps(*pslab(slab), HALO + d, n_half) for d in range(1 - CONV_WIDTH, 2)}
            for par in range(2):
                acc = None
                for j in range(CONV_WIDTH):
                    term = stream[par - (CONV_WIDTH - 1) + j] * cw[j:j + 1, :]
                    acc = term if acc is None else acc + term
                a = _silu(acc)
                if ci < 2 * N_HEADS:
                    a = a * lax.rsqrt(jnp.sum(a * a, axis=-1, keepdims=True) + NORM_EPS)
                if ci < N_HEADS:
                    a = a * (HEAD_DIM ** -0.5)
                qkv_ref[ci, pl.ds(HALO + par, (EXT - HALO) // 2, stride=2), :] = a
        return run

    convs = [conv_group(ci) for ci in range(3 * N_HEADS)]
    pools = [None, pool_pair(0), None, pool_pair(1)]
    first_slab = {name: off // 128 for name, off in
                  dict(q=OFF_Q, k=OFF_K, v=OFF_V, pu=OFF_PU, pz=OFF_PZ, dz=OFF_DZ).items()}
    pair_order = [first_slab[name] + j for name in ("q", "k", "v", "pu", "pz", "dz") for j in (0, 2)][:-1]
    mixers = [None] * N_HEADS + convs + [None] * N_HEADS + pools
    for i, slab in enumerate(pair_order):
        proj_pair(slab)()
        for task in mixers[2 * i:2 * i + 2]:
            if task is not None:
                task()
    for task in mixers[2 * len(pair_order):]:
        if task is not None:
            task()
    consts = _packed_constants()

    def recurrence_chunk(c):
        def run():
            r0 = c * CHUNK
            for h in range(N_HEADS):
                hs = (c // ROW_CHUNKS) * N_HEADS + h
                state = s_ref[hs]
                r = jnp.dot(mq_ref[c, h], state.astype(jnp.bfloat16), preferred_element_type=jnp.float32)
                s_ref[hs] = state * dec_ref[c, h][0:1, :] + n_ref[c, h] - r[:HEAD_DIM]
                o = r[HEAD_DIM:] + oi_ref[c, h]
                o = o * lax.rsqrt(jnp.mean(o * o, axis=-1, keepdims=True) + NORM_EPS) * dn_norm_w_ref[...]
                dz = pdz_ref[h, _slab_row(r0):_slab_row(r0) + CHUNK, :]
                y_ref[r0:r0 + CHUNK, D_POOL + h * HEAD_DIM:D_POOL + (h + 1) * HEAD_DIM] = (
                    o * _silu(dz)).astype(jnp.bfloat16)
        return run

    def output_block():
        hres = x_ref[...].reshape(TS, D_MODEL) + jnp.dot(y_ref[...], wo_ref[...],
                                                         preferred_element_type=jnp.float32)
        out_ref[...] = (hres * lax.rsqrt(jnp.mean(hres * hres, axis=-1, keepdims=True) + NORM_EPS)
                        * fnorm_w_ref[...]).reshape(ROWS, TB, D_MODEL)

    _delta_precompute(range(N_CHUNKS), qkv_ref, gate_ref, gatet_ref,
                      mq_ref, n_ref, oi_ref, dec_ref, consts)
    for cc in range(ROW_CHUNKS):
        for rb in range(ROWS):
            recurrence_chunk(rb * ROW_CHUNKS + cc)()
    output_block()


def _lane_row(vals, lane0):
    return jnp.zeros((1, TAIL), jnp.float32).at[0, lane0:lane0 + vals.shape[0]].set(vals)


def kernel(x, norm_w, w_in, pool_w, pool_scale, conv_w, a_log, dt_bias, dn_norm_w, w_out, final_norm_w):
    B, S, D = x.shape
    assert D == D_MODEL and S % TB == 0 and B % ROWS == 0 and norm_w.shape[0] == 1
    assert w_in.shape == (1, D_MODEL, D_IN)

    def const(shape):
        return pl.BlockSpec(shape, lambda b, s: (0,) * len(shape), pipeline_mode=pl.Buffered(1))

    blk = pl.BlockSpec((ROWS, TB, D_MODEL), lambda b, s: (b, s, 0))
    n_groups = len(POOL_WINDOWS)
    return pl.pallas_call(
        _block_kernel,
        grid=(B // ROWS, S // TB),
        in_specs=[
            blk,
            const((1, D_MODEL)),
            const((1, D_MODEL, D_IN)),
            const((1, n_groups, POOL_GROUP, POOL_GROUP)),
            const((1, D_POOL)),
            const((CONV_WIDTH, 3 * D_DN)),
            const((1, TAIL)),
            const((1, TAIL)),
            const((1, HEAD_DIM)),
            const((1, D_MODEL, D_MODEL)),
            const((1, D_MODEL)),
        ],
        out_specs=blk,
        out_shape=jax.ShapeDtypeStruct(x.shape, x.dtype),
        scratch_shapes=[
            pltpu.VMEM((D_MODEL, D_MAIN + TAIL), jnp.bfloat16),
            pltpu.VMEM((D_MODEL, D_MODEL), jnp.bfloat16),
            pltpu.VMEM((n_groups // 2, 2 * POOL_GROUP, 2 * POOL_GROUP), jnp.bfloat16),
        ] + [pltpu.VMEM((GROUP_SLABS, EXT, 128), jnp.float32)
             for _ in range(N_SLABS // GROUP_SLABS)] + [
            pltpu.VMEM((2, 2, EXT, 128), jnp.float32),
            pltpu.VMEM((3 * N_HEADS, EXT, HEAD_DIM), jnp.float32),
            pltpu.VMEM((TS, TAIL), jnp.float32),
            pltpu.VMEM((TAIL, TS), jnp.float32),
            pltpu.VMEM((TS, D_MODEL), jnp.bfloat16),
            pltpu.VMEM((ROWS * N_HEADS, HEAD_DIM, HEAD_DIM), jnp.float32),
            pltpu.VMEM((N_CHUNKS, N_HEADS, HEAD_DIM + CHUNK, HEAD_DIM), jnp.bfloat16),
            pltpu.VMEM((N_CHUNKS, N_HEADS, HEAD_DIM, HEAD_DIM), jnp.float32),
            pltpu.VMEM((N_CHUNKS, N_HEADS, CHUNK, HEAD_DIM), jnp.float32),
            pltpu.VMEM((N_CHUNKS, N_HEADS, 8, HEAD_DIM), jnp.float32),
        ],
        compiler_params=pltpu.CompilerParams(
            dimension_semantics=("arbitrary", "arbitrary"),
            vmem_limit_bytes=VMEM_LIMIT_BYTES),
        name="hymba_block",
    )(x, norm_w, w_in, pool_w, pool_scale, conv_w[0],
      _lane_row(a_log[0], LANE_DECAY), _lane_row(dt_bias[0], LANE_DECAY),
      dn_norm_w, w_out, final_norm_w[None, :])
```

```python
import jax
import jax.numpy as jnp
from jax import lax
from jax.experimental import pallas as pl
from jax.experimental.pallas import tpu as pltpu

D_MODEL = 1024
D_POOL = 512
D_DN = 512
POOL_WINDOWS = (2, 4, 8, 16)
POOL_GROUP = 128
HEAD_DIM = 128
N_HEADS = 4
CONV_WIDTH = 4
NORM_EPS = 1e-6

D_MAIN = 2 * D_POOL + 4 * D_DN
D_IN = D_MAIN + 2 * N_HEADS
TAIL = 128
OFF_PU, OFF_PZ, OFF_Q, OFF_K, OFF_V, OFF_DZ = 0, 512, 1024, 1536, 2048, 2560
LANE_BETA, LANE_DECAY = 0, N_HEADS

HALO = 24
POOL_EXT = 16
N_SLABS = (2 * D_POOL + 4 * D_DN) // 128
GROUP_SLABS = 4
CHUNK = 64
TS = 512
ROWS = 2
TB = TS // ROWS
SEG = HALO + TB
EXT = ROWS * SEG
N_CHUNKS = TS // CHUNK
ROW_CHUNKS = TB // CHUNK
PROJ_COLS = 256
W_STAGE_ROWS = 128

VMEM_LIMIT_BYTES = 56 * 1024 * 1024


def _bdot(a, b):
    return jnp.dot(a.astype(jnp.bfloat16), b.astype(jnp.bfloat16),
                   preferred_element_type=jnp.float32)


def _bdot_tn(a, b):
    return lax.dot_general(a.astype(jnp.bfloat16), b.astype(jnp.bfloat16),
                           (((0,), (0,)), ((), ())), preferred_element_type=jnp.float32)


def _slab_row(r):
    return (r // TB) * SEG + HALO + r % TB


def _from_slab(ref, lead):
    return jnp.concatenate([ref[lead + (slice(rb * SEG + HALO, (rb + 1) * SEG), slice(None))]
                            for rb in range(ROWS)], axis=0)


def _to_slab(ref, lead, val):
    for rb in range(ROWS):
        ref[lead + (slice(rb * SEG + HALO, (rb + 1) * SEG), slice(None))] = val[rb * TB:(rb + 1) * TB]


def _silu(x):
    h = 0.5 * x
    return h + h * jnp.tanh(h)


def _packed_constants():
    shape = (CHUNK, N_HEADS * CHUNK)
    row = lax.broadcasted_iota(jnp.int32, shape, 0)
    lane = lax.broadcasted_iota(jnp.int32, shape, 1)
    col = lane % CHUNK
    blk = lane // CHUNK
    f32 = jnp.float32
    levels = []
    r = 1
    while r < CHUNK:
        levels.append((((row // (2 * r)) == (col // (2 * r)))
                       & ((row % (2 * r)) >= r) & ((col % (2 * r)) < r)).astype(f32))
        r *= 2
    return dict(
        causal=row >= col,
        strict=(row > col).astype(f32),
        eye=(row == col).astype(f32),
        levels=levels,
        blk=blk,
        head=[(blk == h).astype(jnp.bfloat16) for h in range(N_HEADS)],
    )


def _block_diag(p, consts):
    pb = p.astype(jnp.bfloat16)
    return jnp.concatenate([pb * m for m in consts["head"]], axis=0)


def _delta_precompute(chunks, qkv_ref, gate_ref, gatet_ref, mq_ref, n_ref, oi_ref, dec_ref, consts):
    f32, bf16 = jnp.float32, jnp.bfloat16
    heads = range(N_HEADS)

    def rows(c):
        return slice(c * CHUNK, (c + 1) * CHUNK)

    def srows(c):
        start = _slab_row(c * CHUNK)
        return slice(start, start + CHUNK)

    def lane_row(c, first):
        return jnp.concatenate([gatet_ref[first + h:first + h + 1, rows(c)] for h in heads], axis=1)

    def stack(parts):
        return jnp.concatenate(parts, axis=0)

    a_mats, t_mats, qkbs = [], [], []
    for c in chunks:
        brow = lane_row(c, LANE_BETA)
        grow = lane_row(c, LANE_DECAY)
        gcol = None
        for h in reversed(heads):
            bc = jnp.broadcast_to(gate_ref[rows(c), LANE_DECAY + h:LANE_DECAY + h + 1],
                                  (CHUNK, N_HEADS * CHUNK))
            gcol = bc if gcol is None else jnp.where(consts["blk"] == h, bc, gcol)
        db = jnp.exp(jnp.where(consts["causal"], gcol - grow, -jnp.inf)) * brow
        k16 = [qkv_ref[N_HEADS + h, srows(c), :].astype(bf16) for h in heads]
        q16 = [qkv_ref[h, srows(c), :].astype(bf16) for h in heads]
        zero = jnp.zeros((CHUNK, HEAD_DIM), bf16)
        kdiag = jnp.concatenate(
            [jnp.concatenate([k16[h] if j == h else zero for j in heads], axis=1) for h in heads], axis=0)
        lhs = jnp.concatenate([jnp.concatenate(k16, axis=1), jnp.concatenate(q16, axis=1)], axis=0)
        kq = lax.dot_general(lhs, kdiag, (((1,), (1,)), ((), ())), preferred_element_type=f32)
        a_mat = kq[:CHUNK] * db * consts["strict"]
        a_mats.append(a_mat)
        qkbs.append(kq[CHUNK:] * db)
        t_mats.append(consts["eye"] - a_mat * consts["levels"][0])
    g_mats = [a - jnp.dot(a.astype(bf16), _block_diag(a * consts["levels"][0], consts),
                          preferred_element_type=f32) for a in a_mats]
    for m in consts["levels"][1:]:
        tgs = [jnp.dot(jnp.concatenate([t, g], axis=0).astype(bf16), _block_diag(g * m, consts),
                       preferred_element_type=f32) for t, g in zip(t_mats, g_mats)]
        t_mats = [t - tg[:CHUNK] for t, tg in zip(t_mats, tgs)]
        g_mats = [g - tg[CHUNK:] for g, tg in zip(g_mats, tgs)]
    uws = []
    for c, t in zip(chunks, t_mats):
        v_stack = stack([qkv_ref[2 * N_HEADS + h, srows(c), :] for h in heads])
        kg_stack = stack([qkv_ref[N_HEADS + h, srows(c), :]
                          * jnp.exp(gate_ref[rows(c), LANE_DECAY + h:LANE_DECAY + h + 1]) for h in heads])
        rhs = jnp.concatenate([v_stack, kg_stack], axis=1).astype(bf16)
        uws.append(jnp.dot(_block_diag(t, consts), rhs, preferred_element_type=f32).astype(bf16))
    for c, uw, qkb in zip(chunks, uws, qkbs):
        k_t = stack([qkv_ref[N_HEADS + h, srows(c), :] for h in heads]).T
        glast_row = jnp.concatenate(
            [jnp.broadcast_to(gatet_ref[LANE_DECAY + h:LANE_DECAY + h + 1, (c + 1) * CHUNK - 1:(c + 1) * CHUNK],
                              (1, CHUNK)) for h in heads], axis=1)
        kdb_t = (k_t * (jnp.exp(glast_row - lane_row(c, LANE_DECAY)) * lane_row(c, LANE_BETA))).astype(bf16)
        lhs = jnp.concatenate([_block_diag(qkb, consts)] + [kdb_t * m[0:1, :] for m in consts["head"]], axis=0)
        prod = jnp.dot(lhs, uw, preferred_element_type=f32)
        qw, nm_all = prod[:N_HEADS * CHUNK], prod[N_HEADS * CHUNK:]
        for h in heads:
            hs = slice(h * CHUNK, (h + 1) * CHUNK)
            gcol = gate_ref[rows(c), LANE_DECAY + h:LANE_DECAY + h + 1]
            glast = gate_ref[(c + 1) * CHUNK - 1:(c + 1) * CHUNK, LANE_DECAY + h:LANE_DECAY + h + 1]
            nm = nm_all[h * HEAD_DIM:(h + 1) * HEAD_DIM]
            n_ref[c, h] = nm[:, :HEAD_DIM]
            oi_ref[c, h] = qw[hs, :HEAD_DIM]
            q_eff = qkv_ref[h, srows(c), :] * jnp.exp(gcol) - qw[hs, HEAD_DIM:]
            mq_ref[c, h] = jnp.concatenate([nm[:, HEAD_DIM:], q_eff], axis=0).astype(bf16)
            dec_ref[c, h] = jnp.broadcast_to(jnp.exp(glast), (8, HEAD_DIM))


def _block_kernel(x_ref, norm_w_ref, w_in_t_ref, w_tail_ref, pool_w_ref, pool_scale_ref,
                  conv_w_ref, a_log_ref, dt_bias_ref, dn_norm_w_ref, w_out_ref, fnorm_w_ref,
                  out_ref,
                  w_all_ref, wo_ref, pw_ref,
                  pu_ref, pz_ref, pq_ref, pk_ref, pv_ref, pdz_ref,
                  ps_ref, qkv_ref, gate_ref, gatet_ref, y_ref, s_ref,
                  mq_ref, n_ref, oi_ref, dec_ref):
    sblk = pl.program_id(1)
    first_step = (pl.program_id(0) == 0) & (sblk == 0)
    group_refs = (pu_ref, pz_ref, pq_ref, pk_ref, pv_ref, pdz_ref)

    def pslab(slab):
        return group_refs[slab // GROUP_SLABS], slab % GROUP_SLABS

    @pl.when(first_step)
    def _():
        w_all_ref[:, D_MAIN:] = w_tail_ref[...].astype(jnp.bfloat16)
        for c0 in range(0, D_MAIN, W_STAGE_ROWS):
            cs = slice(c0, c0 + W_STAGE_ROWS)
            w_all_ref[:, cs] = w_in_t_ref[0, cs, :].T.astype(jnp.bfloat16)
        for r0 in range(0, D_MODEL, W_STAGE_ROWS):
            rs = slice(r0, r0 + W_STAGE_ROWS)
            wo_ref[rs, :] = w_out_ref[0, rs, :].astype(jnp.bfloat16)
        pw_ref[...] = jnp.zeros_like(pw_ref)
        for gi in range(len(POOL_WINDOWS)):
            d0 = (gi % 2) * POOL_GROUP
            pw_ref[gi // 2, d0:d0 + POOL_GROUP, d0:d0 + POOL_GROUP] = pool_w_ref[0, gi].astype(jnp.bfloat16)
        ps_ref[:, :, 0:HALO - POOL_EXT, :] = jnp.zeros((2, 2, HALO - POOL_EXT, 128), jnp.float32)

    @pl.when(sblk == 0)
    def _():
        for ref in group_refs:
            for rb in range(ROWS):
                ref[:, rb * SEG:rb * SEG + HALO, :] = jnp.zeros((GROUP_SLABS, HALO, 128), jnp.float32)
        s_ref[...] = jnp.zeros_like(s_ref)

    @pl.when(sblk != 0)
    def _():
        for ref in group_refs:
            for rb in range(ROWS):
                ref[:, rb * SEG:rb * SEG + HALO, :] = ref[:, rb * SEG + TB:(rb + 1) * SEG, :]

    x = x_ref[...].reshape(TS, D_MODEL)
    n = x * lax.rsqrt(jnp.mean(x * x, axis=-1, keepdims=True) + NORM_EPS) * norm_w_ref[...]
    nb = n.astype(jnp.bfloat16)

    res = jnp.dot(nb, w_all_ref[:, D_MAIN - PROJ_COLS:], preferred_element_type=jnp.float32)
    _to_slab(pdz_ref, (GROUP_SLABS - 2,), res[:, :128])
    _to_slab(pdz_ref, (GROUP_SLABS - 1,), res[:, 128:256])
    tail = res[:, 256:]

    gdec = -jnp.exp(a_log_ref[...]) * jax.nn.softplus(tail + dt_bias_ref[...])
    r_in_chunk = lax.broadcasted_iota(jnp.int32, (TS, TAIL), 0) % CHUNK
    gc = gdec
    k = 1
    while k < CHUNK:
        gc = gc + jnp.where(r_in_chunk >= k, pltpu.roll(gc, k, axis=0), 0.0)
        k *= 2
    lane_t = lax.broadcasted_iota(jnp.int32, (TS, TAIL), 1)
    gates = jnp.where(lane_t < LANE_DECAY, jax.nn.sigmoid(tail), gc)
    gate_ref[...] = gates
    gatet_ref[...] = gates.T

    def proj_pair(slab):
        def run():
            c0 = slab * 128
            res = jnp.dot(nb, w_all_ref[:, c0:c0 + PROJ_COLS], preferred_element_type=jnp.float32)
            ref, j = pslab(slab)
            _to_slab(ref, (j,), res[:, :128])
            _to_slab(ref, (j + 1,), res[:, 128:])
        return run

    def taps(ref, slab, first_row, n_rows):
        lead = slab if isinstance(slab, tuple) else (slab,)
        return ref[lead + (pl.ds(first_row, n_rows, stride=2), slice(None))]

    t_pos = sblk * TB + lax.broadcasted_iota(jnp.int32, (TS, 1), 0) % TB

    def window_sum(gi, w):
        n_half = (EXT - (HALO - POOL_EXT)) // 2
        src, src_slab = pu_ref, (gi,)
        k = 1
        level = 0
        while k < w:
            for par in range(2):
                first = HALO - POOL_EXT + par
                ssum = taps(src, src_slab, first, n_half) + taps(src, src_slab, first - k, n_half)
                ps_ref[gi % 2, level % 2, pl.ds(first, n_half, stride=2), :] = ssum
            src, src_slab = ps_ref, (gi % 2, level % 2)
            k *= 2
            level += 1
        return _from_slab(src, src_slab)

    def pool_pair(pair):
        def run():
            mixes = []
            for gi in (2 * pair, 2 * pair + 1):
                w = POOL_WINDOWS[gi]
                inv_cnt = 1.0 / jnp.minimum(t_pos + 1, w).astype(jnp.float32)
                mixes.append(window_sum(gi, w) * inv_cnt - _from_slab(pu_ref, (gi,)))
            mixed = _bdot(jnp.concatenate(mixes, axis=1), pw_ref[pair])
            z = jnp.concatenate([_from_slab(pz_ref, (gi,)) for gi in (2 * pair, 2 * pair + 1)], axis=1)
            c0 = OFF_PU + 2 * pair * POOL_GROUP
            y_ref[:, c0:c0 + 2 * POOL_GROUP] = (
                mixed * pool_scale_ref[:, c0:c0 + 2 * POOL_GROUP] * _silu(z)).astype(jnp.bfloat16)
        return run

    def conv_group(ci):
        def run():
            slab = OFF_Q // 128 + ci
            cw = conv_w_ref[:, ci * HEAD_DIM:(ci + 1) * HEAD_DIM]
            n_half = (EXT - HALO) // 2
            stream = {d: taps(*pslab(slab), HALO + d, n_half) for d in range(1 - CONV_WIDTH, 2)}
            for par in range(2):
                acc = None
                for j in range(CONV_WIDTH):
                    term = stream[par - (CONV_WIDTH - 1) + j] * cw[j:j + 1, :]
                    acc = term if acc is None else acc + term
                a = _silu(acc)
                if ci < 2 * N_HEADS:
                    a = a * lax.rsqrt(jnp.sum(a * a, axis=-1, keepdims=True) + NORM_EPS)
                if ci < N_HEADS:
                    a = a * (HEAD_DIM ** -0.5)
                qkv_ref[ci, pl.ds(HALO + par, (EXT - HALO) // 2, stride=2), :] = a
        return run

    convs = [conv_group(ci) for ci in range(3 * N_HEADS)]
    pools = [None, pool_pair(0), None, pool_pair(1)]
    first_slab = {name: off // 128 for name, off in
                  dict(q=OFF_Q, k=OFF_K, v=OFF_V, pu=OFF_PU, pz=OFF_PZ, dz=OFF_DZ).items()}
    pair_order = [first_slab[name] + j for name in ("q", "k", "v", "pu", "pz", "dz") for j in (0, 2)][:-1]
    mixers = [None] * N_HEADS + convs + [None] * N_HEADS + pools
    for i, slab in enumerate(pair_order):
        proj_pair(slab)()
        for task in mixers[2 * i:2 * i + 2]:
            if task is not None:
                task()
    for task in mixers[2 * len(pair_order):]:
        if task is not None:
            task()
    consts = _packed_constants()

    def recurrence_chunk(c):
        def run():
            r0 = c * CHUNK
            for h in range(N_HEADS):
                hs = (c // ROW_CHUNKS) * N_HEADS + h
                state = s_ref[hs]
                r = jnp.dot(mq_ref[c, h], state.astype(jnp.bfloat16), preferred_element_type=jnp.float32)
                s_ref[hs] = state * dec_ref[c, h][0:1, :] + n_ref[c, h] - r[:HEAD_DIM]
                o = r[HEAD_DIM:] + oi_ref[c, h]
                o = o * lax.rsqrt(jnp.mean(o * o, axis=-1, keepdims=True) + NORM_EPS) * dn_norm_w_ref[...]
                dz = pdz_ref[h, _slab_row(r0):_slab_row(r0) + CHUNK, :]
                y_ref[r0:r0 + CHUNK, D_POOL + h * HEAD_DIM:D_POOL + (h + 1) * HEAD_DIM] = (
                    o * _silu(dz)).astype(jnp.bfloat16)
        return run

    def output_block():
        hres = x_ref[...].reshape(TS, D_MODEL) + jnp.dot(y_ref[...], wo_ref[...],
                                                         preferred_element_type=jnp.float32)
        out_ref[...] = (hres * lax.rsqrt(jnp.mean(hres * hres, axis=-1, keepdims=True) + NORM_EPS)
                        * fnorm_w_ref[...]).reshape(ROWS, TB, D_MODEL)

    _delta_precompute(range(N_CHUNKS), qkv_ref, gate_ref, gatet_ref,
                      mq_ref, n_ref, oi_ref, dec_ref, consts)
    for cc in range(ROW_CHUNKS):
        for rb in range(ROWS):
            recurrence_chunk(rb * ROW_CHUNKS + cc)()
    output_block()


def _lane_row(vals, lane0):
    return jnp.zeros((1, TAIL), jnp.float32).at[0, lane0:lane0 + vals.shape[0]].set(vals)


def kernel(x, norm_w, w_in, pool_w, pool_scale, conv_w, a_log, dt_bias, dn_norm_w, w_out, final_norm_w):
    B, S, D = x.shape
    assert D == D_MODEL and S % TB == 0 and B % ROWS == 0 and norm_w.shape[0] == 1
    assert w_in.shape == (1, D_MODEL, D_IN)

    def const(shape):
        return pl.BlockSpec(shape, lambda b, s: (0,) * len(shape), pipeline_mode=pl.Buffered(1))

    blk = pl.BlockSpec((ROWS, TB, D_MODEL), lambda b, s: (b, s, 0))
    w_tail = jnp.pad(w_in[0, :, D_MAIN:], ((0, 0), (0, TAIL - (D_IN - D_MAIN))))
    n_groups = len(POOL_WINDOWS)
    return pl.pallas_call(
        _block_kernel,
        grid=(B // ROWS, S // TB),
        in_specs=[
            blk,
            const((1, D_MODEL)),
            const((1, D_IN, D_MODEL)),
            const((D_MODEL, TAIL)),
            const((1, n_groups, POOL_GROUP, POOL_GROUP)),
            const((1, D_POOL)),
            const((CONV_WIDTH, 3 * D_DN)),
            const((1, TAIL)),
            const((1, TAIL)),
            const((1, HEAD_DIM)),
            const((1, D_MODEL, D_MODEL)),
            const((1, D_MODEL)),
        ],
        out_specs=blk,
        out_shape=jax.ShapeDtypeStruct(x.shape, x.dtype),
        scratch_shapes=[
            pltpu.VMEM((D_MODEL, D_MAIN + TAIL), jnp.bfloat16),
            pltpu.VMEM((D_MODEL, D_MODEL), jnp.bfloat16),
            pltpu.VMEM((n_groups // 2, 2 * POOL_GROUP, 2 * POOL_GROUP), jnp.bfloat16),
        ] + [pltpu.VMEM((GROUP_SLABS, EXT, 128), jnp.float32)
             for _ in range(N_SLABS // GROUP_SLABS)] + [
            pltpu.VMEM((2, 2, EXT, 128), jnp.float32),
            pltpu.VMEM((3 * N_HEADS, EXT, HEAD_DIM), jnp.float32),
            pltpu.VMEM((TS, TAIL), jnp.float32),
            pltpu.VMEM((TAIL, TS), jnp.float32),
            pltpu.VMEM((TS, D_MODEL), jnp.bfloat16),
            pltpu.VMEM((ROWS * N_HEADS, HEAD_DIM, HEAD_DIM), jnp.float32),
            pltpu.VMEM((N_CHUNKS, N_HEADS, HEAD_DIM + CHUNK, HEAD_DIM), jnp.bfloat16),
            pltpu.VMEM((N_CHUNKS, N_HEADS, HEAD_DIM, HEAD_DIM), jnp.float32),
            pltpu.VMEM((N_CHUNKS, N_HEADS, CHUNK, HEAD_DIM), jnp.float32),
            pltpu.VMEM((N_CHUNKS, N_HEADS, 8, HEAD_DIM), jnp.float32),
        ],
        compiler_params=pltpu.CompilerParams(
            dimension_semantics=("arbitrary", "arbitrary"),
            vmem_limit_bytes=VMEM_LIMIT_BYTES),
        name="hymba_block",
    )(x, norm_w, jnp.swapaxes(w_in, 1, 2), w_tail, pool_w, pool_scale, conv_w[0],
      _lane_row(a_log[0], LANE_DECAY), _lane_row(dt_bias[0], LANE_DECAY),
      dn_norm_w, w_out, final_norm_w[None, :])
```

```python
import jax
import jax.numpy as jnp
from jax import lax
from jax.experimental import pallas as pl
from jax.experimental.pallas import tpu as pltpu

D_MODEL = 1024
D_POOL = 512
D_DN = 512
POOL_WINDOWS = (2, 4, 8, 16)
POOL_GROUP = 128
HEAD_DIM = 128
N_HEADS = 4
CONV_WIDTH = 4
NORM_EPS = 1e-6

D_MAIN = 2 * D_POOL + 4 * D_DN
D_IN = D_MAIN + 2 * N_HEADS
TAIL = 128
OFF_PU, OFF_PZ, OFF_Q, OFF_K, OFF_V, OFF_DZ = 0, 512, 1024, 1536, 2048, 2560
LANE_BETA, LANE_DECAY = 0, N_HEADS

HALO = 24
POOL_EXT = 16
N_SLABS = (2 * D_POOL + 4 * D_DN) // 128
GROUP_SLABS = 4
CHUNK = 64
TS = 512
ROWS = 2
TB = TS // ROWS
SEG = HALO + TB
EXT = ROWS * SEG
N_CHUNKS = TS // CHUNK
ROW_CHUNKS = TB // CHUNK
PROJ_COLS = 256
W_STAGE_ROWS = 128

VMEM_LIMIT_BYTES = 56 * 1024 * 1024


def _bdot(a, b):
    return jnp.dot(a.astype(jnp.bfloat16), b.astype(jnp.bfloat16),
                   preferred_element_type=jnp.float32)


def _bdot_tn(a, b):
    return lax.dot_general(a.astype(jnp.bfloat16), b.astype(jnp.bfloat16),
                           (((0,), (0,)), ((), ())), preferred_element_type=jnp.float32)


def _slab_row(r):
    return (r // TB) * SEG + HALO + r % TB


def _from_slab(ref, lead):
    return jnp.concatenate([ref[lead + (slice(rb * SEG + HALO, (rb + 1) * SEG), slice(None))]
                            for rb in range(ROWS)], axis=0)


def _to_slab(ref, lead, val):
    for rb in range(ROWS):
        ref[lead + (slice(rb * SEG + HALO, (rb + 1) * SEG), slice(None))] = val[rb * TB:(rb + 1) * TB]


def _silu(x):
    h = 0.5 * x
    return h + h * jnp.tanh(h)


def _packed_constants():
    shape = (CHUNK, N_HEADS * CHUNK)
    row = lax.broadcasted_iota(jnp.int32, shape, 0)
    lane = lax.broadcasted_iota(jnp.int32, shape, 1)
    col = lane % CHUNK
    blk = lane // CHUNK
    f32 = jnp.float32
    levels = []
    r = 1
    while r < CHUNK:
        levels.append((((row // (2 * r)) == (col // (2 * r)))
                       & ((row % (2 * r)) >= r) & ((col % (2 * r)) < r)).astype(f32))
        r *= 2
    return dict(
        causal=row >= col,
        strict=(row > col).astype(f32),
        eye=(row == col).astype(f32),
        levels=levels,
        blk=blk,
        head=[(blk == h).astype(jnp.bfloat16) for h in range(N_HEADS)],
    )


def _block_diag(p, consts):
    pb = p.astype(jnp.bfloat16)
    return jnp.concatenate([pb * m for m in consts["head"]], axis=0)


def _delta_precompute(chunks, qkv_ref, gate_ref, gatet_ref, mq_ref, n_ref, oi_ref, dec_ref, consts):
    f32, bf16 = jnp.float32, jnp.bfloat16
    heads = range(N_HEADS)

    def rows(c):
        return slice(c * CHUNK, (c + 1) * CHUNK)

    def srows(c):
        start = _slab_row(c * CHUNK)
        return slice(start, start + CHUNK)

    def lane_row(c, first):
        return jnp.concatenate([gatet_ref[first + h:first + h + 1, rows(c)] for h in heads], axis=1)

    def stack(parts):
        return jnp.concatenate(parts, axis=0)

    a_mats, t_mats, qkbs = [], [], []
    for c in chunks:
        brow = lane_row(c, LANE_BETA)
        grow = lane_row(c, LANE_DECAY)
        gcol = None
        for h in reversed(heads):
            bc = jnp.broadcast_to(gate_ref[rows(c), LANE_DECAY + h:LANE_DECAY + h + 1],
                                  (CHUNK, N_HEADS * CHUNK))
            gcol = bc if gcol is None else jnp.where(consts["blk"] == h, bc, gcol)
        db = jnp.exp(jnp.where(consts["causal"], gcol - grow, -jnp.inf)) * brow
        k16 = [qkv_ref[N_HEADS + h, srows(c), :].astype(bf16) for h in heads]
        q16 = [qkv_ref[h, srows(c), :].astype(bf16) for h in heads]
        zero = jnp.zeros((CHUNK, HEAD_DIM), bf16)
        kdiag = jnp.concatenate(
            [jnp.concatenate([k16[h] if j == h else zero for j in heads], axis=1) for h in heads], axis=0)
        lhs = jnp.concatenate([jnp.concatenate(k16, axis=1), jnp.concatenate(q16, axis=1)], axis=0)
        kq = lax.dot_general(lhs, kdiag, (((1,), (1,)), ((), ())), preferred_element_type=f32)
        a_mat = kq[:CHUNK] * db * consts["strict"]
        a_mats.append(a_mat)
        qkbs.append(kq[CHUNK:] * db)
        t_mats.append(consts["eye"] - a_mat * consts["levels"][0])
    g_mats = [a - jnp.dot(a.astype(bf16), _block_diag(a * consts["levels"][0], consts),
                          preferred_element_type=f32) for a in a_mats]
    for m in consts["levels"][1:]:
        tgs = [jnp.dot(jnp.concatenate([t, g], axis=0).astype(bf16), _block_diag(g * m, consts),
                       preferred_element_type=f32) for t, g in zip(t_mats, g_mats)]
        t_mats = [t - tg[:CHUNK] for t, tg in zip(t_mats, tgs)]
        g_mats = [g - tg[CHUNK:] for g, tg in zip(g_mats, tgs)]
    uws = []
    for c, t in zip(chunks, t_mats):
        v_stack = stack([qkv_ref[2 * N_HEADS + h, srows(c), :] for h in heads])
        kg_stack = stack([qkv_ref[N_HEADS + h, srows(c), :]
                          * jnp.exp(gate_ref[rows(c), LANE_DECAY + h:LANE_DECAY + h + 1]) for h in heads])
        rhs = jnp.concatenate([v_stack, kg_stack], axis=1).astype(bf16)
        uws.append(jnp.dot(_block_diag(t, consts), rhs, preferred_element_type=f32).astype(bf16))
    for c, uw, qkb in zip(chunks, uws, qkbs):
        k_t = stack([qkv_ref[N_HEADS + h, srows(c), :] for h in heads]).T
        glast_row = jnp.concatenate(
            [jnp.broadcast_to(gatet_ref[LANE_DECAY + h:LANE_DECAY + h + 1, (c + 1) * CHUNK - 1:(c + 1) * CHUNK],
                              (1, CHUNK)) for h in heads], axis=1)
        kdb_t = (k_t * (jnp.exp(glast_row - lane_row(c, LANE_DECAY)) * lane_row(c, LANE_BETA))).astype(bf16)
        lhs = jnp.concatenate([_block_diag(qkb, consts)] + [kdb_t * m[0:1, :] for m in consts["head"]], axis=0)
        prod = jnp.dot(lhs, uw, preferred_element_type=f32)
        qw, nm_all = prod[:N_HEADS * CHUNK], prod[N_HEADS * CHUNK:]
        for h in heads:
            hs = slice(h * CHUNK, (h + 1) * CHUNK)
            gcol = gate_ref[rows(c), LANE_DECAY + h:LANE_DECAY + h + 1]
            glast = gate_ref[(c + 1) * CHUNK - 1:(c + 1) * CHUNK, LANE_DECAY + h:LANE_DECAY + h + 1]
            nm = nm_all[h * HEAD_DIM:(h + 1) * HEAD_DIM]
            n_ref[c, h] = nm[:, :HEAD_DIM]
            oi_ref[c, h] = qw[hs, :HEAD_DIM]
            q_eff = qkv_ref[h, srows(c), :] * jnp.exp(gcol) - qw[hs, HEAD_DIM:]
            mq_ref[c, h] = jnp.concatenate([nm[:, HEAD_DIM:], q_eff], axis=0).astype(bf16)
            dec_ref[c, h] = jnp.broadcast_to(jnp.exp(glast), (8, HEAD_DIM))


def _block_kernel(x_ref, norm_w_ref, w_in_t_ref, pool_w_ref, pool_scale_ref,
                  conv_w_ref, gate_par_ref, dn_norm_w_ref, w_out_ref, fnorm_w_ref,
                  out_ref,
                  w_all_ref, wo_ref, pw_ref,
                  pu_ref, pz_ref, pq_ref, pk_ref, pv_ref, pdz_ref,
                  ps_ref, qkv_ref, gate_ref, gatet_ref, y_ref, s_ref,
                  mq_ref, n_ref, oi_ref, dec_ref):
    sblk = pl.program_id(1)
    first_step = (pl.program_id(0) == 0) & (sblk == 0)
    group_refs = (pu_ref, pz_ref, pq_ref, pk_ref, pv_ref, pdz_ref)

    def pslab(slab):
        return group_refs[slab // GROUP_SLABS], slab % GROUP_SLABS

    @pl.when(first_step)
    def _():
        lane = lax.broadcasted_iota(jnp.int32, (D_MODEL, TAIL), 1)
        last = pltpu.roll(w_in_t_ref[0, D_IN - TAIL:D_IN, :].T, D_IN - D_MAIN, axis=1)
        w_all_ref[:, D_MAIN:] = jnp.where(lane < D_IN - D_MAIN, last, 0.0).astype(jnp.bfloat16)
        for c0 in range(0, D_MAIN, W_STAGE_ROWS):
            cs = slice(c0, c0 + W_STAGE_ROWS)
            w_all_ref[:, cs] = w_in_t_ref[0, cs, :].T.astype(jnp.bfloat16)
        for r0 in range(0, D_MODEL, W_STAGE_ROWS):
            rs = slice(r0, r0 + W_STAGE_ROWS)
            wo_ref[rs, :] = w_out_ref[0, rs, :].astype(jnp.bfloat16)
        pw_ref[...] = jnp.zeros_like(pw_ref)
        for gi in range(len(POOL_WINDOWS)):
            d0 = (gi % 2) * POOL_GROUP
            pw_ref[gi // 2, d0:d0 + POOL_GROUP, d0:d0 + POOL_GROUP] = pool_w_ref[0, gi].astype(jnp.bfloat16)
        ps_ref[:, :, 0:HALO - POOL_EXT, :] = jnp.zeros((2, 2, HALO - POOL_EXT, 128), jnp.float32)

    @pl.when(sblk == 0)
    def _():
        for ref in group_refs:
            for rb in range(ROWS):
                ref[:, rb * SEG:rb * SEG + HALO, :] = jnp.zeros((GROUP_SLABS, HALO, 128), jnp.float32)
        s_ref[...] = jnp.zeros_like(s_ref)

    @pl.when(sblk != 0)
    def _():
        for ref in group_refs:
            for rb in range(ROWS):
                ref[:, rb * SEG:rb * SEG + HALO, :] = ref[:, rb * SEG + TB:(rb + 1) * SEG, :]

    x = x_ref[...].reshape(TS, D_MODEL)
    n = x * lax.rsqrt(jnp.mean(x * x, axis=-1, keepdims=True) + NORM_EPS) * norm_w_ref[...]
    nb = n.astype(jnp.bfloat16)

    res = jnp.dot(nb, w_all_ref[:, D_MAIN - PROJ_COLS:], preferred_element_type=jnp.float32)
    _to_slab(pdz_ref, (GROUP_SLABS - 2,), res[:, :128])
    _to_slab(pdz_ref, (GROUP_SLABS - 1,), res[:, 128:256])
    tail = res[:, 256:]

    gdec = -jnp.exp(gate_par_ref[0:1, :]) * jax.nn.softplus(tail + gate_par_ref[1:2, :])
    r_in_chunk = lax.broadcasted_iota(jnp.int32, (TS, TAIL), 0) % CHUNK
    gc = gdec
    k = 1
    while k < CHUNK:
        gc = gc + jnp.where(r_in_chunk >= k, pltpu.roll(gc, k, axis=0), 0.0)
        k *= 2
    lane_t = lax.broadcasted_iota(jnp.int32, (TS, TAIL), 1)
    gates = jnp.where(lane_t < LANE_DECAY, jax.nn.sigmoid(tail), gc)
    gate_ref[...] = gates
    gatet_ref[...] = gates.T

    def proj_pair(slab):
        def run():
            c0 = slab * 128
            res = jnp.dot(nb, w_all_ref[:, c0:c0 + PROJ_COLS], preferred_element_type=jnp.float32)
            ref, j = pslab(slab)
            _to_slab(ref, (j,), res[:, :128])
            _to_slab(ref, (j + 1,), res[:, 128:])
        return run

    def taps(ref, slab, first_row, n_rows):
        lead = slab if isinstance(slab, tuple) else (slab,)
        return ref[lead + (pl.ds(first_row, n_rows, stride=2), slice(None))]

    t_pos = sblk * TB + lax.broadcasted_iota(jnp.int32, (TS, 1), 0) % TB

    def window_sum(gi, w):
        n_half = (EXT - (HALO - POOL_EXT)) // 2
        src, src_slab = pu_ref, (gi,)
        k = 1
        level = 0
        while k < w:
            for par in range(2):
                first = HALO - POOL_EXT + par
                ssum = taps(src, src_slab, first, n_half) + taps(src, src_slab, first - k, n_half)
                ps_ref[gi % 2, level % 2, pl.ds(first, n_half, stride=2), :] = ssum
            src, src_slab = ps_ref, (gi % 2, level % 2)
            k *= 2
            level += 1
        return _from_slab(src, src_slab)

    def pool_pair(pair):
        def run():
            mixes = []
            for gi in (2 * pair, 2 * pair + 1):
                w = POOL_WINDOWS[gi]
                inv_cnt = 1.0 / jnp.minimum(t_pos + 1, w).astype(jnp.float32)
                mixes.append(window_sum(gi, w) * inv_cnt - _from_slab(pu_ref, (gi,)))
            mixed = _bdot(jnp.concatenate(mixes, axis=1), pw_ref[pair])
            z = jnp.concatenate([_from_slab(pz_ref, (gi,)) for gi in (2 * pair, 2 * pair + 1)], axis=1)
            c0 = OFF_PU + 2 * pair * POOL_GROUP
            y_ref[:, c0:c0 + 2 * POOL_GROUP] = (
                mixed * pool_scale_ref[:, c0:c0 + 2 * POOL_GROUP] * _silu(z)).astype(jnp.bfloat16)
        return run

    def conv_group(ci):
        def run():
            slab = OFF_Q // 128 + ci
            cw = conv_w_ref[0, :, ci * HEAD_DIM:(ci + 1) * HEAD_DIM]
            n_half = (EXT - HALO) // 2
            stream = {d: taps(*pslab(slab), HALO + d, n_half) for d in range(1 - CONV_WIDTH, 2)}
            for par in range(2):
                acc = None
                for j in range(CONV_WIDTH):
                    term = stream[par - (CONV_WIDTH - 1) + j] * cw[j:j + 1, :]
                    acc = term if acc is None else acc + term
                a = _silu(acc)
                if ci < 2 * N_HEADS:
                    a = a * lax.rsqrt(jnp.sum(a * a, axis=-1, keepdims=True) + NORM_EPS)
                if ci < N_HEADS:
                    a = a * (HEAD_DIM ** -0.5)
                qkv_ref[ci, pl.ds(HALO + par, (EXT - HALO) // 2, stride=2), :] = a
        return run

    convs = [conv_group(ci) for ci in range(3 * N_HEADS)]
    pools = [None, pool_pair(0), None, pool_pair(1)]
    first_slab = {name: off // 128 for name, off in
                  dict(q=OFF_Q, k=OFF_K, v=OFF_V, pu=OFF_PU, pz=OFF_PZ, dz=OFF_DZ).items()}
    pair_order = [first_slab[name] + j for name in ("q", "k", "v", "pu", "pz", "dz") for j in (0, 2)][:-1]
    mixers = [None] * N_HEADS + convs + [None] * N_HEADS + pools
    for i, slab in enumerate(pair_order):
        proj_pair(slab)()
        for task in mixers[2 * i:2 * i + 2]:
            if task is not None:
                task()
    for task in mixers[2 * len(pair_order):]:
        if task is not None:
            task()
    consts = _packed_constants()

    def recurrence_chunk(c):
        def run():
            r0 = c * CHUNK
            for h in range(N_HEADS):
                hs = (c // ROW_CHUNKS) * N_HEADS + h
                state = s_ref[hs]
                r = jnp.dot(mq_ref[c, h], state.astype(jnp.bfloat16), preferred_element_type=jnp.float32)
                s_ref[hs] = state * dec_ref[c, h][0:1, :] + n_ref[c, h] - r[:HEAD_DIM]
                o = r[HEAD_DIM:] + oi_ref[c, h]
                o = o * lax.rsqrt(jnp.mean(o * o, axis=-1, keepdims=True) + NORM_EPS) * dn_norm_w_ref[...]
                dz = pdz_ref[h, _slab_row(r0):_slab_row(r0) + CHUNK, :]
                y_ref[r0:r0 + CHUNK, D_POOL + h * HEAD_DIM:D_POOL + (h + 1) * HEAD_DIM] = (
                    o * _silu(dz)).astype(jnp.bfloat16)
        return run

    def output_block():
        hres = x_ref[...].reshape(TS, D_MODEL) + jnp.dot(y_ref[...], wo_ref[...],
                                                         preferred_element_type=jnp.float32)
        out_ref[...] = (hres * lax.rsqrt(jnp.mean(hres * hres, axis=-1, keepdims=True) + NORM_EPS)
                        * fnorm_w_ref[...]).reshape(ROWS, TB, D_MODEL)

    _delta_precompute(range(N_CHUNKS), qkv_ref, gate_ref, gatet_ref,
                      mq_ref, n_ref, oi_ref, dec_ref, consts)
    for cc in range(ROW_CHUNKS):
        for rb in range(ROWS):
            recurrence_chunk(rb * ROW_CHUNKS + cc)()
    output_block()


def kernel(x, norm_w, w_in, pool_w, pool_scale, conv_w, a_log, dt_bias, dn_norm_w, w_out, final_norm_w):
    B, S, D = x.shape
    assert D == D_MODEL and S % TB == 0 and B % ROWS == 0 and norm_w.shape[0] == 1
    assert w_in.shape == (1, D_MODEL, D_IN)

    def const(shape):
        return pl.BlockSpec(shape, lambda b, s: (0,) * len(shape), pipeline_mode=pl.Buffered(1))

    blk = pl.BlockSpec((ROWS, TB, D_MODEL), lambda b, s: (b, s, 0))
    gate_par = jnp.pad(jnp.concatenate([a_log, dt_bias], axis=0),
                       ((0, 0), (LANE_DECAY, TAIL - LANE_DECAY - a_log.shape[1])))
    n_groups = len(POOL_WINDOWS)
    return pl.pallas_call(
        _block_kernel,
        grid=(B // ROWS, S // TB),
        in_specs=[
            blk,
            const((1, D_MODEL)),
            const((1, D_IN, D_MODEL)),
            const((1, n_groups, POOL_GROUP, POOL_GROUP)),
            const((1, D_POOL)),
            const((1, CONV_WIDTH, 3 * D_DN)),
            const((2, TAIL)),
            const((1, HEAD_DIM)),
            const((1, D_MODEL, D_MODEL)),
            const((1, D_MODEL)),
        ],
        out_specs=blk,
        out_shape=jax.ShapeDtypeStruct(x.shape, x.dtype),
        scratch_shapes=[
            pltpu.VMEM((D_MODEL, D_MAIN + TAIL), jnp.bfloat16),
            pltpu.VMEM((D_MODEL, D_MODEL), jnp.bfloat16),
            pltpu.VMEM((n_groups // 2, 2 * POOL_GROUP, 2 * POOL_GROUP), jnp.bfloat16),
        ] + [pltpu.VMEM((GROUP_SLABS, EXT, 128), jnp.float32)
             for _ in range(N_SLABS // GROUP_SLABS)] + [
            pltpu.VMEM((2, 2, EXT, 128), jnp.float32),
            pltpu.VMEM((3 * N_HEADS, EXT, HEAD_DIM), jnp.float32),
            pltpu.VMEM((TS, TAIL), jnp.float32),
            pltpu.VMEM((TAIL, TS), jnp.float32),
            pltpu.VMEM((TS, D_MODEL), jnp.bfloat16),
            pltpu.VMEM((ROWS * N_HEADS, HEAD_DIM, HEAD_DIM), jnp.float32),
            pltpu.VMEM((N_CHUNKS, N_HEADS, HEAD_DIM + CHUNK, HEAD_DIM), jnp.bfloat16),
            pltpu.VMEM((N_CHUNKS, N_HEADS, HEAD_DIM, HEAD_DIM), jnp.float32),
            pltpu.VMEM((N_CHUNKS, N_HEADS, CHUNK, HEAD_DIM), jnp.float32),
            pltpu.VMEM((N_CHUNKS, N_HEADS, 8, HEAD_DIM), jnp.float32),
        ],
        compiler_params=pltpu.CompilerParams(
            dimension_semantics=("arbitrary", "arbitrary"),
            vmem_limit_bytes=VMEM_LIMIT_BYTES),
        name="hymba_block",
    )(x, norm_w, jnp.swapaxes(w_in, 1, 2), pool_w, pool_scale, conv_w,
      gate_par, dn_norm_w, w_out, final_norm_w[None, :])
```

```python
import jax
import jax.numpy as jnp
from jax import lax
from jax.experimental import pallas as pl
from jax.experimental.pallas import tpu as pltpu

D_MODEL = 1024
D_POOL = 512
D_DN = 512
POOL_WINDOWS = (2, 4, 8, 16)
POOL_GROUP = 128
HEAD_DIM = 128
N_HEADS = 4
CONV_WIDTH = 4
NORM_EPS = 1e-6

D_MAIN = 2 * D_POOL + 4 * D_DN
D_IN = D_MAIN + 2 * N_HEADS
TAIL = 128
OFF_PU, OFF_PZ, OFF_Q, OFF_K, OFF_V, OFF_DZ = 0, 512, 1024, 1536, 2048, 2560
LANE_BETA, LANE_DECAY = 0, N_HEADS

HALO = 24
POOL_EXT = 16
N_SLABS = (2 * D_POOL + 4 * D_DN) // 128
GROUP_SLABS = 4
CHUNK = 64
TS = 512
ROWS = 2
TB = TS // ROWS
SEG = HALO + TB
EXT = ROWS * SEG
N_CHUNKS = TS // CHUNK
ROW_CHUNKS = TB // CHUNK
PROJ_COLS = 256
W_STAGE_ROWS = 128

VMEM_LIMIT_BYTES = 56 * 1024 * 1024


def _bdot(a, b):
    return jnp.dot(a.astype(jnp.bfloat16), b.astype(jnp.bfloat16),
                   preferred_element_type=jnp.float32)


def _bdot_tn(a, b):
    return lax.dot_general(a.astype(jnp.bfloat16), b.astype(jnp.bfloat16),
                           (((0,), (0,)), ((), ())), preferred_element_type=jnp.float32)


def _slab_row(r):
    return (r // TB) * SEG + HALO + r % TB


def _from_slab(ref, lead):
    return jnp.concatenate([ref[lead + (slice(rb * SEG + HALO, (rb + 1) * SEG), slice(None))]
                            for rb in range(ROWS)], axis=0)


def _to_slab(ref, lead, val):
    for rb in range(ROWS):
        ref[lead + (slice(rb * SEG + HALO, (rb + 1) * SEG), slice(None))] = val[rb * TB:(rb + 1) * TB]


def _silu(x):
    h = 0.5 * x
    return h + h * jnp.tanh(h)


def _packed_constants():
    shape = (CHUNK, N_HEADS * CHUNK)
    row = lax.broadcasted_iota(jnp.int32, shape, 0)
    lane = lax.broadcasted_iota(jnp.int32, shape, 1)
    col = lane % CHUNK
    blk = lane // CHUNK
    f32 = jnp.float32
    levels = []
    r = 1
    while r < CHUNK:
        levels.append((((row // (2 * r)) == (col // (2 * r)))
                       & ((row % (2 * r)) >= r) & ((col % (2 * r)) < r)).astype(f32))
        r *= 2
    return dict(
        causal=row >= col,
        strict=(row > col).astype(f32),
        eye=(row == col).astype(f32),
        levels=levels,
        blk=blk,
        head=[(blk == h).astype(jnp.bfloat16) for h in range(N_HEADS)],
    )


def _block_diag(p, consts):
    pb = p.astype(jnp.bfloat16)
    return jnp.concatenate([pb * m for m in consts["head"]], axis=0)


def _delta_precompute(chunks, qkv_ref, gate_ref, gatet_ref, mq_ref, n_ref, oi_ref, dec_ref, consts):
    f32, bf16 = jnp.float32, jnp.bfloat16
    heads = range(N_HEADS)

    def rows(c):
        return slice(c * CHUNK, (c + 1) * CHUNK)

    def srows(c):
        start = _slab_row(c * CHUNK)
        return slice(start, start + CHUNK)

    def lane_row(c, first):
        return jnp.concatenate([gatet_ref[first + h:first + h + 1, rows(c)] for h in heads], axis=1)

    def stack(parts):
        return jnp.concatenate(parts, axis=0)

    a_mats, t_mats, qkbs = [], [], []
    for c in chunks:
        brow = lane_row(c, LANE_BETA)
        grow = lane_row(c, LANE_DECAY)
        gcol = None
        for h in reversed(heads):
            bc = jnp.broadcast_to(gate_ref[rows(c), LANE_DECAY + h:LANE_DECAY + h + 1],
                                  (CHUNK, N_HEADS * CHUNK))
            gcol = bc if gcol is None else jnp.where(consts["blk"] == h, bc, gcol)
        db = jnp.exp(jnp.where(consts["causal"], gcol - grow, -jnp.inf)) * brow
        k16 = [qkv_ref[N_HEADS + h, srows(c), :].astype(bf16) for h in heads]
        q16 = [qkv_ref[h, srows(c), :].astype(bf16) for h in heads]
        zero = jnp.zeros((CHUNK, HEAD_DIM), bf16)
        kdiag = jnp.concatenate(
            [jnp.concatenate([k16[h] if j == h else zero for j in heads], axis=1) for h in heads], axis=0)
        lhs = jnp.concatenate([jnp.concatenate(k16, axis=1), jnp.concatenate(q16, axis=1)], axis=0)
        kq = lax.dot_general(lhs, kdiag, (((1,), (1,)), ((), ())), preferred_element_type=f32)
        a_mat = kq[:CHUNK] * db * consts["strict"]
        a_mats.append(a_mat)
        qkbs.append(kq[CHUNK:] * db)
        t_mats.append(consts["eye"] - a_mat * consts["levels"][0])
    g_mats = [a - jnp.dot(a.astype(bf16), _block_diag(a * consts["levels"][0], consts),
                          preferred_element_type=f32) for a in a_mats]
    for m in consts["levels"][1:]:
        tgs = [jnp.dot(jnp.concatenate([t, g], axis=0).astype(bf16), _block_diag(g * m, consts),
                       preferred_element_type=f32) for t, g in zip(t_mats, g_mats)]
        t_mats = [t - tg[:CHUNK] for t, tg in zip(t_mats, tgs)]
        g_mats = [g - tg[CHUNK:] for g, tg in zip(g_mats, tgs)]
    uws = []
    for c, t in zip(chunks, t_mats):
        v_stack = stack([qkv_ref[2 * N_HEADS + h, srows(c), :] for h in heads])
        kg_stack = stack([qkv_ref[N_HEADS + h, srows(c), :]
                          * jnp.exp(gate_ref[rows(c), LANE_DECAY + h:LANE_DECAY + h + 1]) for h in heads])
        rhs = jnp.concatenate([v_stack, kg_stack], axis=1).astype(bf16)
        uws.append(jnp.dot(_block_diag(t, consts), rhs, preferred_element_type=f32).astype(bf16))
    for c, uw, qkb in zip(chunks, uws, qkbs):
        k_t = stack([qkv_ref[N_HEADS + h, srows(c), :] for h in heads]).T
        glast_row = jnp.concatenate(
            [jnp.broadcast_to(gatet_ref[LANE_DECAY + h:LANE_DECAY + h + 1, (c + 1) * CHUNK - 1:(c + 1) * CHUNK],
                              (1, CHUNK)) for h in heads], axis=1)
        kdb_t = (k_t * (jnp.exp(glast_row - lane_row(c, LANE_DECAY)) * lane_row(c, LANE_BETA))).astype(bf16)
        lhs = jnp.concatenate([_block_diag(qkb, consts)] + [kdb_t * m[0:1, :] for m in consts["head"]], axis=0)
        prod = jnp.dot(lhs, uw, preferred_element_type=f32)
        qw, nm_all = prod[:N_HEADS * CHUNK], prod[N_HEADS * CHUNK:]
        for h in heads:
            hs = slice(h * CHUNK, (h + 1) * CHUNK)
            gcol = gate_ref[rows(c), LANE_DECAY + h:LANE_DECAY + h + 1]
            glast = gate_ref[(c + 1) * CHUNK - 1:(c + 1) * CHUNK, LANE_DECAY + h:LANE_DECAY + h + 1]
            nm = nm_all[h * HEAD_DIM:(h + 1) * HEAD_DIM]
            n_ref[c, h] = nm[:, :HEAD_DIM]
            oi_ref[c, h] = qw[hs, :HEAD_DIM]
            q_eff = qkv_ref[h, srows(c), :] * jnp.exp(gcol) - qw[hs, HEAD_DIM:]
            mq_ref[c, h] = jnp.concatenate([nm[:, HEAD_DIM:], q_eff], axis=0).astype(bf16)
            dec_ref[c, h] = jnp.broadcast_to(jnp.exp(glast), (8, HEAD_DIM))


def _block_kernel(x_ref, norm_w_ref, w_in_t_ref, pool_w_ref, pool_scale_ref,
                  conv_w_ref, a_log_ref, dt_bias_ref, dn_norm_w_ref, w_out_ref, fnorm_w_ref,
                  out_ref,
                  w_all_ref, wo_ref, pw_ref,
                  pu_ref, pz_ref, pq_ref, pk_ref, pv_ref, pdz_ref,
                  ps_ref, qkv_ref, gate_ref, gatet_ref, y_ref, s_ref,
                  mq_ref, n_ref, oi_ref, dec_ref):
    sblk = pl.program_id(1)
    first_step = (pl.program_id(0) == 0) & (sblk == 0)
    group_refs = (pu_ref, pz_ref, pq_ref, pk_ref, pv_ref, pdz_ref)

    def pslab(slab):
        return group_refs[slab // GROUP_SLABS], slab % GROUP_SLABS

    @pl.when(first_step)
    def _():
        lane = lax.broadcasted_iota(jnp.int32, (D_MODEL, TAIL), 1)
        last = pltpu.roll(w_in_t_ref[0, D_IN - TAIL:D_IN, :].T, D_IN - D_MAIN, axis=1)
        w_all_ref[:, D_MAIN:] = jnp.where(lane < D_IN - D_MAIN, last, 0.0).astype(jnp.bfloat16)
        for c0 in range(0, D_MAIN, W_STAGE_ROWS):
            cs = slice(c0, c0 + W_STAGE_ROWS)
            w_all_ref[:, cs] = w_in_t_ref[0, cs, :].T.astype(jnp.bfloat16)
        for r0 in range(0, D_MODEL, W_STAGE_ROWS):
            rs = slice(r0, r0 + W_STAGE_ROWS)
            wo_ref[rs, :] = w_out_ref[0, rs, :].astype(jnp.bfloat16)
        pw_ref[...] = jnp.zeros_like(pw_ref)
        for gi in range(len(POOL_WINDOWS)):
            d0 = (gi % 2) * POOL_GROUP
            pw_ref[gi // 2, d0:d0 + POOL_GROUP, d0:d0 + POOL_GROUP] = pool_w_ref[0, gi].astype(jnp.bfloat16)
        ps_ref[:, :, 0:HALO - POOL_EXT, :] = jnp.zeros((2, 2, HALO - POOL_EXT, 128), jnp.float32)

    @pl.when(sblk == 0)
    def _():
        for ref in group_refs:
            for rb in range(ROWS):
                ref[:, rb * SEG:rb * SEG + HALO, :] = jnp.zeros((GROUP_SLABS, HALO, 128), jnp.float32)
        s_ref[...] = jnp.zeros_like(s_ref)

    @pl.when(sblk != 0)
    def _():
        for ref in group_refs:
            for rb in range(ROWS):
                ref[:, rb * SEG:rb * SEG + HALO, :] = ref[:, rb * SEG + TB:(rb + 1) * SEG, :]

    x = x_ref[...].reshape(TS, D_MODEL)
    n = x * lax.rsqrt(jnp.mean(x * x, axis=-1, keepdims=True) + NORM_EPS) * norm_w_ref[...]
    nb = n.astype(jnp.bfloat16)

    res = jnp.dot(nb, w_all_ref[:, D_MAIN - PROJ_COLS:], preferred_element_type=jnp.float32)
    _to_slab(pdz_ref, (GROUP_SLABS - 2,), res[:, :128])
    _to_slab(pdz_ref, (GROUP_SLABS - 1,), res[:, 128:256])
    tail = res[:, 256:]

    par_lane = lax.broadcasted_iota(jnp.int32, (1, TAIL), 1)
    a_row = jnp.zeros((1, TAIL), jnp.float32)
    dt_row = jnp.zeros((1, TAIL), jnp.float32)
    for h in range(N_HEADS):
        a_row = jnp.where(par_lane == LANE_DECAY + h, a_log_ref[0, h], a_row)
        dt_row = jnp.where(par_lane == LANE_DECAY + h, dt_bias_ref[0, h], dt_row)
    gdec = -jnp.exp(a_row) * jax.nn.softplus(tail + dt_row)
    r_in_chunk = lax.broadcasted_iota(jnp.int32, (TS, TAIL), 0) % CHUNK
    gc = gdec
    k = 1
    while k < CHUNK:
        gc = gc + jnp.where(r_in_chunk >= k, pltpu.roll(gc, k, axis=0), 0.0)
        k *= 2
    lane_t = lax.broadcasted_iota(jnp.int32, (TS, TAIL), 1)
    gates = jnp.where(lane_t < LANE_DECAY, jax.nn.sigmoid(tail), gc)
    gate_ref[...] = gates
    gatet_ref[...] = gates.T

    def proj_pair(slab):
        def run():
            c0 = slab * 128
            res = jnp.dot(nb, w_all_ref[:, c0:c0 + PROJ_COLS], preferred_element_type=jnp.float32)
            ref, j = pslab(slab)
            _to_slab(ref, (j,), res[:, :128])
            _to_slab(ref, (j + 1,), res[:, 128:])
        return run

    def taps(ref, slab, first_row, n_rows):
        lead = slab if isinstance(slab, tuple) else (slab,)
        return ref[lead + (pl.ds(first_row, n_rows, stride=2), slice(None))]

    t_pos = sblk * TB + lax.broadcasted_iota(jnp.int32, (TS, 1), 0) % TB

    def window_sum(gi, w):
        n_half = (EXT - (HALO - POOL_EXT)) // 2
        src, src_slab = pu_ref, (gi,)
        k = 1
        level = 0
        while k < w:
            for par in range(2):
                first = HALO - POOL_EXT + par
                ssum = taps(src, src_slab, first, n_half) + taps(src, src_slab, first - k, n_half)
                ps_ref[gi % 2, level % 2, pl.ds(first, n_half, stride=2), :] = ssum
            src, src_slab = ps_ref, (gi % 2, level % 2)
            k *= 2
            level += 1
        return _from_slab(src, src_slab)

    def pool_pair(pair):
        def run():
            mixes = []
            for gi in (2 * pair, 2 * pair + 1):
                w = POOL_WINDOWS[gi]
                inv_cnt = 1.0 / jnp.minimum(t_pos + 1, w).astype(jnp.float32)
                mixes.append(window_sum(gi, w) * inv_cnt - _from_slab(pu_ref, (gi,)))
            mixed = _bdot(jnp.concatenate(mixes, axis=1), pw_ref[pair])
            z = jnp.concatenate([_from_slab(pz_ref, (gi,)) for gi in (2 * pair, 2 * pair + 1)], axis=1)
            c0 = OFF_PU + 2 * pair * POOL_GROUP
            y_ref[:, c0:c0 + 2 * POOL_GROUP] = (
                mixed * pool_scale_ref[:, c0:c0 + 2 * POOL_GROUP] * _silu(z)).astype(jnp.bfloat16)
        return run

    def conv_group(ci):
        def run():
            slab = OFF_Q // 128 + ci
            cw = conv_w_ref[0, :, ci * HEAD_DIM:(ci + 1) * HEAD_DIM]
            n_half = (EXT - HALO) // 2
            stream = {d: taps(*pslab(slab), HALO + d, n_half) for d in range(1 - CONV_WIDTH, 2)}
            for par in range(2):
                acc = None
                for j in range(CONV_WIDTH):
                    term = stream[par - (CONV_WIDTH - 1) + j] * cw[j:j + 1, :]
                    acc = term if acc is None else acc + term
                a = _silu(acc)
                if ci < 2 * N_HEADS:
                    a = a * lax.rsqrt(jnp.sum(a * a, axis=-1, keepdims=True) + NORM_EPS)
                if ci < N_HEADS:
                    a = a * (HEAD_DIM ** -0.5)
                qkv_ref[ci, pl.ds(HALO + par, (EXT - HALO) // 2, stride=2), :] = a
        return run

    convs = [conv_group(ci) for ci in range(3 * N_HEADS)]
    pools = [None, pool_pair(0), None, pool_pair(1)]
    first_slab = {name: off // 128 for name, off in
                  dict(q=OFF_Q, k=OFF_K, v=OFF_V, pu=OFF_PU, pz=OFF_PZ, dz=OFF_DZ).items()}
    pair_order = [first_slab[name] + j for name in ("q", "k", "v", "pu", "pz", "dz") for j in (0, 2)][:-1]
    mixers = [None] * N_HEADS + convs + [None] * N_HEADS + pools
    for i, slab in enumerate(pair_order):
        proj_pair(slab)()
        for task in mixers[2 * i:2 * i + 2]:
            if task is not None:
                task()
    for task in mixers[2 * len(pair_order):]:
        if task is not None:
            task()
    consts = _packed_constants()

    def recurrence_chunk(c):
        def run():
            r0 = c * CHUNK
            for h in range(N_HEADS):
                hs = (c // ROW_CHUNKS) * N_HEADS + h
                state = s_ref[hs]
                r = jnp.dot(mq_ref[c, h], state.astype(jnp.bfloat16), preferred_element_type=jnp.float32)
                s_ref[hs] = state * dec_ref[c, h][0:1, :] + n_ref[c, h] - r[:HEAD_DIM]
                o = r[HEAD_DIM:] + oi_ref[c, h]
                o = o * lax.rsqrt(jnp.mean(o * o, axis=-1, keepdims=True) + NORM_EPS) * dn_norm_w_ref[...]
                dz = pdz_ref[h, _slab_row(r0):_slab_row(r0) + CHUNK, :]
                y_ref[r0:r0 + CHUNK, D_POOL + h * HEAD_DIM:D_POOL + (h + 1) * HEAD_DIM] = (
                    o * _silu(dz)).astype(jnp.bfloat16)
        return run

    def output_block():
        hres = x_ref[...].reshape(TS, D_MODEL) + jnp.dot(y_ref[...], wo_ref[...],
                                                         preferred_element_type=jnp.float32)
        out_ref[...] = (hres * lax.rsqrt(jnp.mean(hres * hres, axis=-1, keepdims=True) + NORM_EPS)
                        * fnorm_w_ref[...]).reshape(ROWS, TB, D_MODEL)

    _delta_precompute(range(N_CHUNKS), qkv_ref, gate_ref, gatet_ref,
                      mq_ref, n_ref, oi_ref, dec_ref, consts)
    for cc in range(ROW_CHUNKS):
        for rb in range(ROWS):
            recurrence_chunk(rb * ROW_CHUNKS + cc)()
    output_block()


def kernel(x, norm_w, w_in, pool_w, pool_scale, conv_w, a_log, dt_bias, dn_norm_w, w_out, final_norm_w):
    B, S, D = x.shape
    assert D == D_MODEL and S % TB == 0 and B % ROWS == 0 and norm_w.shape[0] == 1
    assert w_in.shape == (1, D_MODEL, D_IN)

    def const(shape):
        return pl.BlockSpec(shape, lambda b, s: (0,) * len(shape), pipeline_mode=pl.Buffered(1))

    blk = pl.BlockSpec((ROWS, TB, D_MODEL), lambda b, s: (b, s, 0))
    n_groups = len(POOL_WINDOWS)
    return pl.pallas_call(
        _block_kernel,
        grid=(B // ROWS, S // TB),
        in_specs=[
            blk,
            const((1, D_MODEL)),
            const((1, D_IN, D_MODEL)),
            const((1, n_groups, POOL_GROUP, POOL_GROUP)),
            const((1, D_POOL)),
            const((1, CONV_WIDTH, 3 * D_DN)),
            pl.BlockSpec(memory_space=pltpu.SMEM),
            pl.BlockSpec(memory_space=pltpu.SMEM),
            const((1, HEAD_DIM)),
            const((1, D_MODEL, D_MODEL)),
            const((1, D_MODEL)),
        ],
        out_specs=blk,
        out_shape=jax.ShapeDtypeStruct(x.shape, x.dtype),
        scratch_shapes=[
            pltpu.VMEM((D_MODEL, D_MAIN + TAIL), jnp.bfloat16),
            pltpu.VMEM((D_MODEL, D_MODEL), jnp.bfloat16),
            pltpu.VMEM((n_groups // 2, 2 * POOL_GROUP, 2 * POOL_GROUP), jnp.bfloat16),
        ] + [pltpu.VMEM((GROUP_SLABS, EXT, 128), jnp.float32)
             for _ in range(N_SLABS // GROUP_SLABS)] + [
            pltpu.VMEM((2, 2, EXT, 128), jnp.float32),
            pltpu.VMEM((3 * N_HEADS, EXT, HEAD_DIM), jnp.float32),
            pltpu.VMEM((TS, TAIL), jnp.float32),
            pltpu.VMEM((TAIL, TS), jnp.float32),
            pltpu.VMEM((TS, D_MODEL), jnp.bfloat16),
            pltpu.VMEM((ROWS * N_HEADS, HEAD_DIM, HEAD_DIM), jnp.float32),
            pltpu.VMEM((N_CHUNKS, N_HEADS, HEAD_DIM + CHUNK, HEAD_DIM), jnp.bfloat16),
            pltpu.VMEM((N_CHUNKS, N_HEADS, HEAD_DIM, HEAD_DIM), jnp.float32),
            pltpu.VMEM((N_CHUNKS, N_HEADS, CHUNK, HEAD_DIM), jnp.float32),
            pltpu.VMEM((N_CHUNKS, N_HEADS, 8, HEAD_DIM), jnp.float32),
        ],
        compiler_params=pltpu.CompilerParams(
            dimension_semantics=("arbitrary", "arbitrary"),
            vmem_limit_bytes=VMEM_LIMIT_BYTES),
        name="hymba_block",
    )(x, norm_w, jnp.swapaxes(w_in, 1, 2), pool_w, pool_scale, conv_w,
      a_log, dt_bias, dn_norm_w, w_out, final_norm_w[None, :])
```
